```python
import math
import jax, jax.numpy as jnp
from jax import lax
import numpy as np

D_MODEL = 1024
BATCH = 8
SEQ = 4096
DEPTH = 4

PLE_DIM = 256
D_FF = 4 * D_MODEL
N_EVEN = (DEPTH + 1) // 2
N_ODD = DEPTH // 2
EPS = 1e-6
DIFF_DH = 64
DIFF_HEADS = D_MODEL // 256
DIFF_WIDTH = DIFF_HEADS * 2 * DIFF_DH
SWA_DH = 64
SWA_WIDTH = D_MODEL - DIFF_WIDTH
SWA_HEADS = SWA_WIDTH // SWA_DH
DILATED_CONFIGS = ((128, 1), (512, 4), (2048, 16))
Q_BLOCK = 128
SSM_GROUP_CH = 16
SSM_STATE = 64
SSM_WIDTH = D_MODEL // 2
SSM_GROUPS = SSM_WIDTH // SSM_GROUP_CH
CONV_WIDTH = D_MODEL - SSM_WIDTH
CONV_K = 3
EVEN_IN = 3 * DIFF_WIDTH + 3 * SWA_WIDTH
ODD_IN = SSM_WIDTH + 3 * CONV_WIDTH

kernel_name = 'hybrid_diffattn_dilated_s5_shortconv_trunk'


def rms_norm(x, g, eps=EPS):
    xf = x.astype(jnp.float32)
    y = xf * lax.rsqrt(jnp.mean(xf * xf, axis=-1, keepdims=True) + eps) * g.astype(jnp.float32)
    return y.astype(x.dtype)


def diff_attention(q, k, v, lam, sub_gain, lam_init):
    b, s, h = q.shape[:3]
    nb = s // Q_BLOCK
    scale = DIFF_DH ** -0.5
    kf = k.astype(jnp.float32)
    vf = v.astype(jnp.float32)
    qb = jnp.moveaxis(q.astype(jnp.float32).reshape(b, nb, Q_BLOCK, h, 2, DIFF_DH), 1, 0)
    kpos = jnp.arange(s)

    def block(args):
        q_blk, n = args
        sc = jnp.einsum('bqhcd,bkhcd->bhcqk', q_blk, kf) * scale
        qpos = n * Q_BLOCK + jnp.arange(Q_BLOCK)
        sc = jnp.where(kpos[None, :] <= qpos[:, None], sc, -jnp.inf)
        a = jax.nn.softmax(sc, axis=-1)
        w = a[:, :, 0] - lam * a[:, :, 1]
        return jnp.einsum('bhqk,bkhe->bqhe', w, vf)

    o = lax.map(block, (qb, jnp.arange(nb)))
    o = jnp.moveaxis(o, 0, 1).reshape(b, s, h, 2 * DIFF_DH)
    o = o * lax.rsqrt(jnp.mean(o * o, axis=-1, keepdims=True) + 1e-5) * sub_gain.astype(jnp.float32)
    o = o * (1.0 - lam_init)
    return o.reshape(b, s, h * 2 * DIFF_DH)


def strided_window_attention(q, k, v, n_back, dil):
    b, s, h, dh = q.shape
    L = s // dil
    blk = n_back
    nb = -(-L // blk)
    pad_end = nb * blk - L

    def sub(t):
        return t.astype(jnp.float32).reshape(b, L, dil, h, dh)

    qs = jnp.pad(sub(q), ((0, 0), (0, pad_end), (0, 0), (0, 0), (0, 0))).reshape(b, nb, blk, dil, h, dh)

    def band(t):
        tp = jnp.pad(sub(t), ((0, 0), (blk, pad_end), (0, 0), (0, 0), (0, 0))).reshape(b, nb + 1, blk, dil, h, dh)
        return jnp.concatenate([tp[:, :-1], tp[:, 1:]], axis=2)

    kb, vb = band(k), band(v)
    sc = jnp.einsum('bnqrhd,bnkrhd->bnrhqk', qs, kb) * dh ** -0.5
    qi = jnp.arange(blk)[:, None]
    kk = jnp.arange(2 * blk)[None, :]
    dist = blk + qi - kk
    kpos = (jnp.arange(nb)[:, None, None] - 1) * blk + kk[None]
    valid = (dist >= 0) & (dist <= n_back) & (kpos >= 0)
    sc = jnp.where(valid[None, :, None, None], sc, -jnp.inf)
    m = jnp.max(sc, axis=-1)
    e = jnp.exp(sc - m[..., None])
    den = jnp.sum(e, axis=-1)
    o = jnp.einsum('bnrhqk,bnkrhd->bnqrhd', e, vb) / jnp.moveaxis(den, -1, 2)[..., None]
    o = o.reshape(b, nb * blk, dil, h, dh)[:, :L].reshape(b, s, h, dh)
    m = jnp.moveaxis(m, -1, 2).reshape(b, nb * blk, dil, h)[:, :L].reshape(b, s, h)
    den = jnp.moveaxis(den, -1, 2).reshape(b, nb * blk, dil, h)[:, :L].reshape(b, s, h)
    return o, m, den


def dilated_attention(q, k, v):
    b, s, h, dh = q.shape
    outs = [strided_window_attention(q, k, v, w // d, d) for (w, d) in DILATED_CONFIGS]
    m_all = jnp.max(jnp.stack([m for (_, m, _) in outs]), axis=0)
    wts = [den * jnp.exp(m - m_all) for (_, m, den) in outs]
    num = sum(wt[..., None] * o for wt, (o, _, _) in zip(wts, outs))
    tot = sum(wts)
    return (num / tot[..., None]).reshape(b, s, h * dh)


def attention_mixer(h, w_in, w_out, lq1, lk1, lq2, lk2, sub_gain, lam_init):
    b, s, _ = h.shape
    proj = h @ w_in
    cuts = list(np.cumsum([DIFF_WIDTH] * 3 + [SWA_WIDTH] * 2))
    aq, ak, av, bq, bk, bv = jnp.split(proj, cuts, axis=-1)
    f32 = jnp.float32
    lam = (jnp.exp(jnp.sum(lq1.astype(f32) * lk1.astype(f32))) - jnp.exp(jnp.sum(lq2.astype(f32) * lk2.astype(f32)))
           + lam_init)
    ya = diff_attention(aq.reshape(b, s, DIFF_HEADS, 2, DIFF_DH), ak.reshape(b, s, DIFF_HEADS, 2, DIFF_DH),
                        av.reshape(b, s, DIFF_HEADS, 2 * DIFF_DH), lam, sub_gain, lam_init)
    yb = dilated_attention(bq.reshape(b, s, SWA_HEADS, SWA_DH), bk.reshape(b, s, SWA_HEADS, SWA_DH),
                           bv.reshape(b, s, SWA_HEADS, SWA_DH))
    y = jnp.concatenate([ya, yb], axis=-1).astype(h.dtype)
    return y @ w_out


def s5_mixer(u, lam_re, lam_im, log_dt, b_re, b_im, c_re, c_im, d_skip, w_glu):
    f32 = jnp.float32
    bsz, s, _ = u.shape
    uf = u.astype(f32).reshape(bsz, s, SSM_GROUPS, SSM_GROUP_CH)
    lr, li = lam_re.astype(f32), lam_im.astype(f32)
    dt = jnp.exp(log_dt.astype(f32))[:, None]
    mag = jnp.exp(lr * dt)
    abar_re, abar_im = mag * jnp.cos(li * dt), mag * jnp.sin(li * dt)
    den = lr * lr + li * li
    nr, ni = abar_re - 1.0, abar_im
    coef_re = (nr * lr + ni * li) / den
    coef_im = (ni * lr - nr * li) / den
    br, bi = b_re.astype(f32), b_im.astype(f32)
    bb_re = coef_re[..., None] * br - coef_im[..., None] * bi
    bb_im = coef_re[..., None] * bi + coef_im[..., None] * br
    bu_re = jnp.einsum('bsgc,gpc->bsgp', uf, bb_re)
    bu_im = jnp.einsum('bsgc,gpc->bsgp', uf, bb_im)
    a_re = jnp.broadcast_to(abar_re, (1, s, SSM_GROUPS, SSM_STATE))
    a_im = jnp.broadcast_to(abar_im, (1, s, SSM_GROUPS, SSM_STATE))

    def combine(e1, e2):
        a1r, a1i, b1r, b1i = e1
        a2r, a2i, b2r, b2i = e2
        return (a1r * a2r - a1i * a2i, a1r * a2i + a1i * a2r,
                a2r * b1r - a2i * b1i + b2r, a2r * b1i + a2i * b1r + b2i)

    _, _, xr, xi = lax.associative_scan(combine, (a_re, a_im, bu_re, bu_im), axis=1)
    y = (jnp.einsum('bsgp,gcp->bsgc', xr, c_re.astype(f32)) - jnp.einsum('bsgp,gcp->bsgc', xi, c_im.astype(f32))
         + d_skip.astype(f32) * uf)
    y = jax.nn.gelu(y.reshape(bsz, s, SSM_WIDTH))
    return y * jax.nn.sigmoid(y @ w_glu.astype(f32))


def short_conv_mixer(gb, gc, xt, conv_w):
    s = xt.shape[1]
    z = gc * xt
    zp = jnp.pad(z, ((0, 0), (CONV_K - 1, 0), (0, 0)))
    y = sum(conv_w[j] * zp[:, CONV_K - 1 - j: CONV_K - 1 - j + s] for j in range(CONV_K))
    return gb * y


def recurrent_conv_mixer(h, w_in, w_out, lam_re, lam_im, log_dt, b_re, b_im, c_re, c_im, d_skip, w_glu, conv_w):
    proj = h @ w_in
    u, gb, gc, xt = jnp.split(proj, [SSM_WIDTH, SSM_WIDTH + CONV_WIDTH, SSM_WIDTH + 2 * CONV_WIDTH], axis=-1)
    yc = s5_mixer(u, lam_re, lam_im, log_dt, b_re, b_im, c_re, c_im, d_skip, w_glu)
    yd = short_conv_mixer(gb, gc, xt, conv_w)
    y = jnp.concatenate([yc.astype(h.dtype), yd.astype(h.dtype)], axis=-1)
    return y @ w_out


def setup_inputs(seed: int = 0) -> dict:
    key = jax.random.key(seed)
    ks = iter(jax.random.split(key, 32))

    def nrm(shape, scale):
        return jax.random.normal(next(ks), shape, jnp.float32) * scale

    def gain(shape):
        return 1.0 + nrm(shape, 0.02)

    return {
        'x': nrm((BATCH, SEQ, D_MODEL), 1.0),
        'p': nrm((DEPTH, BATCH, SEQ, PLE_DIM), 1.0),
        'norm_mix': gain((DEPTH, D_MODEL)),
        'norm_mlp': gain((DEPTH, D_MODEL)),
        'norm_ple': gain((DEPTH, D_MODEL)),
        'w_mlp_in': nrm((DEPTH, D_MODEL, D_FF), D_MODEL ** -0.5),
        'w_mlp_out': nrm((DEPTH, D_FF, D_MODEL), D_FF ** -0.5),
        'w_ple_proj': nrm((DEPTH, PLE_DIM, D_MODEL), PLE_DIM ** -0.5),
        'w_ple_gate': nrm((DEPTH, D_MODEL, D_MODEL), D_MODEL ** -0.5),
        'attn_w_in': nrm((N_EVEN, D_MODEL, EVEN_IN), D_MODEL ** -0.5),
        'attn_w_out': nrm((N_EVEN, DIFF_WIDTH + SWA_WIDTH, D_MODEL), D_MODEL ** -0.5),
        'diff_lq1': nrm((N_EVEN, DIFF_DH), 0.1),
        'diff_lk1': nrm((N_EVEN, DIFF_DH), 0.1),
        'diff_lq2': nrm((N_EVEN, DIFF_DH), 0.1),
        'diff_lk2': nrm((N_EVEN, DIFF_DH), 0.1),
        'diff_sub_gain': gain((N_EVEN, 2 * DIFF_DH)),
        'rc_w_in': nrm((N_ODD, D_MODEL, ODD_IN), D_MODEL ** -0.5),
        'rc_w_out': nrm((N_ODD, SSM_WIDTH + CONV_WIDTH, D_MODEL), D_MODEL ** -0.5),
        'ssm_lambda_re': -0.5 + nrm((N_ODD, SSM_GROUPS, SSM_STATE), 0.01),
        'ssm_lambda_im': math.pi * jnp.arange(SSM_STATE, dtype=jnp.float32) + nrm((N_ODD, SSM_GROUPS, SSM_STATE), 0.01),
        'ssm_log_dt': jax.random.uniform(next(ks), (N_ODD, SSM_GROUPS), jnp.float32, math.log(1e-3), math.log(1e-1)),
        'ssm_b_re': nrm((N_ODD, SSM_GROUPS, SSM_STATE, SSM_GROUP_CH), (2 * SSM_GROUP_CH) ** -0.5),
        'ssm_b_im': nrm((N_ODD, SSM_GROUPS, SSM_STATE, SSM_GROUP_CH), (2 * SSM_GROUP_CH) ** -0.5),
        'ssm_c_re': nrm((N_ODD, SSM_GROUPS, SSM_GROUP_CH, SSM_STATE), (2 * SSM_STATE) ** -0.5),
        'ssm_c_im': nrm((N_ODD, SSM_GROUPS, SSM_GROUP_CH, SSM_STATE), (2 * SSM_STATE) ** -0.5),
        'ssm_d': nrm((N_ODD, SSM_GROUPS, SSM_GROUP_CH), 1.0),
        'ssm_w_glu': nrm((N_ODD, SSM_WIDTH, SSM_WIDTH), SSM_WIDTH ** -0.5),
        'conv_w': nrm((N_ODD, CONV_K, CONV_WIDTH), CONV_K ** -0.5),
        'norm_final': gain((D_MODEL,)),
    }


def reference(x, p, norm_mix, norm_mlp, norm_ple, w_mlp_in, w_mlp_out, w_ple_proj, w_ple_gate,
              attn_w_in, attn_w_out, diff_lq1, diff_lk1, diff_lq2, diff_lk2, diff_sub_gain,
              rc_w_in, rc_w_out, ssm_lambda_re, ssm_lambda_im, ssm_log_dt, ssm_b_re, ssm_b_im,
              ssm_c_re, ssm_c_im, ssm_d, ssm_w_glu, conv_w, norm_final):
    h = x
    for i in range(DEPTH):
        hn = rms_norm(h, norm_mix[i])
        if i % 2 == 0:
            e = i // 2
            lam_init = 0.8 - 0.6 * math.exp(-0.3 * i)
            y = attention_mixer(hn, attn_w_in[e], attn_w_out[e], diff_lq1[e], diff_lk1[e], diff_lq2[e],
                                diff_lk2[e], diff_sub_gain[e], lam_init)
        else:
            o = i // 2
            y = recurrent_conv_mixer(hn, rc_w_in[o], rc_w_out[o], ssm_lambda_re[o], ssm_lambda_im[o],
                                     ssm_log_dt[o], ssm_b_re[o], ssm_b_im[o], ssm_c_re[o], ssm_c_im[o],
                                     ssm_d[o], ssm_w_glu[o], conv_w[o])
        h = h + y.astype(h.dtype)
        hn = rms_norm(h, norm_mlp[i])
        h = h + jnp.square(jax.nn.relu(hn @ w_mlp_in[i])) @ w_mlp_out[i]
        hn = rms_norm(h, norm_ple[i])
        h = h + (p[i] @ w_ple_proj[i]) * jax.nn.sigmoid(hn @ w_ple_gate[i])
    return rms_norm(h, norm_final)
```

```python
import functools
import math

import jax
import jax.numpy as jnp
from jax import lax
from jax.experimental import pallas as pl
from jax.experimental.pallas import tpu as pltpu

F32 = jnp.float32
BF16 = jnp.bfloat16

EPS = 1e-6
PLE_DIM = 256
DIFF_DH = 64
DIFF_HEADS = 4
DIFF_WIDTH = 512
SWA_DH = 64
SWA_WIDTH = 512
DILATIONS = (16, 4, 1)
N_BACK = 128
DIL_TILE = N_BACK * 16
SSM_GROUP_CH = 16
SSM_STATE = 64
SSM_WIDTH = 512
SSM_GROUPS = 32
SSM_CHUNK = 32
CONV_WIDTH = 512
LANES = 128
NEG_BIG = -1e30
VMEM_LIMIT = 60 * 1024 * 1024


def _cparams(sem):
    return pltpu.CompilerParams(dimension_semantics=sem, vmem_limit_bytes=VMEM_LIMIT)


def _resident(shape):
    nd = len(shape)
    return pl.BlockSpec(shape, lambda *_: (0,) * nd, pipeline_mode=pl.Buffered(1))


def _rms(x, g, eps=EPS):
    return x * lax.rsqrt(jnp.mean(x * x, axis=-1, keepdims=True) + eps) * g


def _sigmoid(x):
    return 1.0 / (1.0 + jnp.exp(-x))


def _gelu_tanh(x):
    c = math.sqrt(2.0 / math.pi)
    return x * (0.5 * (1.0 + jnp.tanh(c * (x + 0.044715 * (x * x * x)))))


def _norm_matmul_kernel(h_ref, g_ref, w_ref, *o_refs, splits):
    xn = _rms(h_ref[...], g_ref[...]).astype(BF16)
    for o_ref, (c0, c1) in zip(o_refs, splits):
        o_ref[...] = jnp.dot(xn, w_ref[:, c0:c1], preferred_element_type=F32).astype(o_ref.dtype)


def _norm_matmul(h, g, w, splits, dtypes, tm):
    t, d = h.shape
    n = w.shape[1]
    return pl.pallas_call(
        functools.partial(_norm_matmul_kernel, splits=splits),
        grid=(t // tm,),
        in_specs=[pl.BlockSpec((tm, d), lambda i: (i, 0)), _resident((1, d)), _resident((d, n))],
        out_specs=[pl.BlockSpec((tm, c1 - c0), lambda i: (i, 0)) for (c0, c1) in splits],
        out_shape=[jax.ShapeDtypeStruct((t, c1 - c0), dt) for (c0, c1), dt in zip(splits, dtypes)],
        compiler_params=_cparams(("parallel",)),
        name="norm_matmul",
    )(h, g.reshape(1, d), w)


def _post_kernel(h_ref, y1_ref, y2_ref, wo_ref, gm_ref, w1_ref, w2_ref, gp_ref, p_ref, wp_ref, wg_ref,
                 gf_ref, o_ref, *, ff_chunk, final):
    k1 = y1_ref.shape[1]
    h = h_ref[...]
    h = h + jnp.dot(y1_ref[...], wo_ref[:k1, :], preferred_element_type=F32)
    h = h + jnp.dot(y2_ref[...], wo_ref[k1:, :], preferred_element_type=F32)
    hn = _rms(h, gm_ref[...]).astype(BF16)
    d_ff = w1_ref.shape[1]
    acc = jnp.zeros_like(h)
    for c in range(d_ff // ff_chunk):
        a = jnp.dot(hn, w1_ref[:, c * ff_chunk:(c + 1) * ff_chunk], preferred_element_type=F32)
        a = jnp.square(jnp.maximum(a, 0.0)).astype(BF16)
        acc = acc + jnp.dot(a, w2_ref[c * ff_chunk:(c + 1) * ff_chunk, :], preferred_element_type=F32)
    h = h + acc
    hn = _rms(h, gp_ref[...]).astype(BF16)
    gate = _sigmoid(jnp.dot(hn, wg_ref[...], preferred_element_type=F32))
    emb = jnp.dot(p_ref[...].astype(BF16), wp_ref[...], preferred_element_type=F32)
    h = h + emb * gate
    if final:
        h = _rms(h, gf_ref[...])
    o_ref[...] = h


def _post(h, y1, y2, wo, gm, w1, w2, gp, p, wp, wg, gf, final, tm):
    t, d = h.shape
    row = lambda width: pl.BlockSpec((tm, width), lambda i: (i, 0))
    return pl.pallas_call(
        functools.partial(_post_kernel, ff_chunk=1024, final=final),
        grid=(t // tm,),
        in_specs=[row(d), row(y1.shape[1]), row(y2.shape[1]), _resident(wo.shape), _resident((1, d)),
                  _resident(w1.shape), _resident(w2.shape), _resident((1, d)), row(p.shape[1]),
                  _resident(wp.shape), _resident(wg.shape), _resident((1, d))],
        out_specs=row(d),
        out_shape=jax.ShapeDtypeStruct((t, d), F32),
        compiler_params=_cparams(("parallel",)),
        name="post_mixer",
    )(h, y1, y2, wo, gm.reshape(1, d), w1, w2, gp.reshape(1, d), p, wp, wg, gf.reshape(1, d))


def _stack_heads(x, scale):
    lane = lax.broadcasted_iota(jnp.int32, x.shape, 1)
    zero = jnp.zeros_like(x)
    lo = jnp.where(lane < 64, x, zero)
    hi = jnp.where(lane >= 64, x, zero)
    return jnp.concatenate([lo, hi], axis=0) * scale


def _diff_attn_kernel(q_ref, k_ref, v_ref, lq1_ref, lk1_ref, lq2_ref, lk2_ref, sg_ref, o_ref, *, tq, lam_init):
    qi = pl.program_id(2)
    qs = _stack_heads(q_ref[...], jnp.asarray(DIFF_DH ** -0.5, BF16))
    rows = 2 * tq

    def step(j, carry, masked):
        m, l, acc = carry
        start = pl.multiple_of(j * tq, tq)
        k = k_ref[pl.ds(start, tq), :]
        v = v_ref[pl.ds(start, tq), :]
        s = lax.dot_general(qs, k, (((1,), (1,)), ((), ())), preferred_element_type=F32)
        if masked:
            r = lax.broadcasted_iota(jnp.int32, s.shape, 0)
            qpos = jnp.where(r >= tq, r - tq, r)
            c = lax.broadcasted_iota(jnp.int32, s.shape, 1)
            s = jnp.where(c <= qpos, s, NEG_BIG)
        m_new = jnp.maximum(m, jnp.max(s, axis=-1, keepdims=True))
        alpha = jnp.exp(m - m_new)
        p = jnp.exp(s - m_new)
        l = alpha * l + jnp.sum(p, axis=-1, keepdims=True)
        acc = alpha * acc + jnp.dot(p.astype(BF16), v, preferred_element_type=F32)
        return m_new, l, acc

    init = (jnp.full((rows, 1), NEG_BIG, F32), jnp.zeros((rows, 1), F32), jnp.zeros((rows, LANES), F32))
    carry = lax.fori_loop(0, qi, lambda j, c: step(j, c, False), init)
    m, l, acc = step(qi, carry, True)
    o = acc / l
    lam = (jnp.exp(jnp.sum(lq1_ref[...] * lk1_ref[...], keepdims=True))
           - jnp.exp(jnp.sum(lq2_ref[...] * lk2_ref[...], keepdims=True)) + lam_init)
    o = o[:tq] - lam * o[tq:]
    o = o * lax.rsqrt(jnp.mean(o * o, axis=-1, keepdims=True) + 1e-5) * sg_ref[...]
    o_ref[...] = (o * (1.0 - lam_init)).astype(o_ref.dtype)


def _diff_attn(proj, lq1, lk1, lq2, lk2, sub_gain, lam_init, tq):
    b, s, _ = proj.shape
    vec = lambda a: a.astype(F32).reshape(1, -1)
    return pl.pallas_call(
        functools.partial(_diff_attn_kernel, tq=tq, lam_init=lam_init),
        grid=(b, DIFF_HEADS, s // tq),
        in_specs=[pl.BlockSpec((None, tq, LANES), lambda bi, h, qi: (bi, qi, h)),
                  pl.BlockSpec((None, s, LANES), lambda bi, h, qi: (bi, 0, DIFF_HEADS + h)),
                  pl.BlockSpec((None, s, LANES), lambda bi, h, qi: (bi, 0, 2 * DIFF_HEADS + h)),
                  _resident((1, DIFF_DH)), _resident((1, DIFF_DH)), _resident((1, DIFF_DH)),
                  _resident((1, DIFF_DH)), _resident((1, 2 * DIFF_DH))],
        out_specs=pl.BlockSpec((None, tq, LANES), lambda bi, h, qi: (bi, qi, h)),
        out_shape=jax.ShapeDtypeStruct((b, s, DIFF_WIDTH), BF16),
        compiler_params=_cparams(("parallel", "parallel", "arbitrary")),
        name="diff_attn",
    )(proj, proj, proj, vec(lq1), vec(lk1), vec(lq2), vec(lk2), vec(sub_gain))


def _dil_attn_kernel(q_ref, kp_ref, kc_ref, vp_ref, vc_ref, o_ref, k_s, v_s, acc_s, m_s, l_s):
    first = pl.program_id(2) == 0
    tile = q_ref.shape[0]
    k_s[pl.ds(0, tile), :] = kp_ref[...]
    k_s[pl.ds(tile, tile), :] = kc_ref[...]
    v_s[pl.ds(0, tile), :] = vp_ref[...]
    v_s[pl.ds(tile, tile), :] = vc_ref[...]
    scale = jnp.asarray(SWA_DH ** -0.5, BF16)
    lane = lax.broadcasted_iota(jnp.int32, (N_BACK, LANES), 1)
    lo = lane < 64
    r = lax.broadcasted_iota(jnp.int32, (2 * N_BACK, 2 * N_BACK), 0)
    qidx = jnp.where(r >= N_BACK, r - N_BACK, r)
    kk = lax.broadcasted_iota(jnp.int32, (2 * N_BACK, 2 * N_BACK), 1)
    band = (kk >= qidx) & (kk <= qidx + N_BACK)

    for dil in DILATIONS:
        per_row = DIL_TILE // (N_BACK * dil)

        def sub_block(n, _, dil=dil, per_row=per_row):
            st = (n % dil) + (n // dil) * (N_BACK * dil) if dil > 1 else n * N_BACK
            if dil == 1:
                st = pl.multiple_of(st, N_BACK)
            rows = pl.ds(st, N_BACK, stride=dil) if dil > 1 else pl.ds(st, N_BACK)
            kbase = tile + st - N_BACK * dil
            krows = pl.ds(kbase, 2 * N_BACK, stride=dil) if dil > 1 else pl.ds(kbase, 2 * N_BACK)
            qs = _stack_heads(q_ref[rows, :].astype(BF16), scale)
            kb = k_s[krows, :].astype(BF16)
            vb = v_s[krows, :].astype(BF16)
            s = lax.dot_general(qs, kb, (((1,), (1,)), ((), ())), preferred_element_type=F32)
            in_seq = jnp.logical_or(jnp.logical_not(first), kbase + dil * kk >= tile)
            s = jnp.where(band & in_seq, s, NEG_BIG)
            m2 = jnp.max(s, axis=-1, keepdims=True)
            p = jnp.exp(s - m2)
            l2 = jnp.sum(p, axis=-1, keepdims=True)
            pv = jnp.dot(p.astype(BF16), vb, preferred_element_type=F32)
            acc_c = jnp.where(lo, pv[:N_BACK], pv[N_BACK:])
            m_c = jnp.where(lo, m2[:N_BACK], m2[N_BACK:])
            l_c = jnp.where(lo, l2[:N_BACK], l2[N_BACK:])
            if dil == DILATIONS[0]:
                acc_s[rows, :] = acc_c
                m_s[rows, :] = m_c
                l_s[rows, :] = l_c
            else:
                m_o = m_s[rows, :]
                m_n = jnp.maximum(m_o, m_c)
                a_o = jnp.exp(m_o - m_n)
                a_c = jnp.exp(m_c - m_n)
                acc_n = a_o * acc_s[rows, :] + a_c * acc_c
                l_n = a_o * l_s[rows, :] + a_c * l_c
                if dil == DILATIONS[-1]:
                    o_ref[rows, :] = (acc_n / l_n).astype(o_ref.dtype)
                else:
                    acc_s[rows, :] = acc_n
                    m_s[rows, :] = m_n
                    l_s[rows, :] = l_n
            return 0

        lax.fori_loop(0, DIL_TILE // N_BACK, sub_block, 0)


def _dil_attn(proj):
    b, s, _ = proj.shape
    pairs = SWA_WIDTH // LANES
    cur = lambda part: pl.BlockSpec((None, DIL_TILE, LANES), lambda bi, hp, i: (bi, i, part * pairs + hp))
    prev = lambda part: pl.BlockSpec((None, DIL_TILE, LANES),
                                     lambda bi, hp, i: (bi, jnp.maximum(i - 1, 0), part * pairs + hp))
    return pl.pallas_call(
        _dil_attn_kernel,
        grid=(b, pairs, s // DIL_TILE),
        in_specs=[cur(0), prev(1), cur(1), prev(2), cur(2)],
        out_specs=pl.BlockSpec((None, DIL_TILE, LANES), lambda bi, hp, i: (bi, i, hp)),
        out_shape=jax.ShapeDtypeStruct((b, s, SWA_WIDTH), BF16),
        scratch_shapes=[pltpu.VMEM((2 * DIL_TILE, LANES), F32), pltpu.VMEM((2 * DIL_TILE, LANES), F32),
                        pltpu.VMEM((DIL_TILE, LANES), F32), pltpu.VMEM((DIL_TILE, LANES), F32),
                        pltpu.VMEM((DIL_TILE, LANES), F32)],
        compiler_params=_cparams(("parallel", "parallel", "arbitrary")),
        name="dilated_attn",
    )(proj, proj, proj, proj, proj)


def _s5_operators(lam_re, lam_im, log_dt, b_re, b_im, c_re, c_im, d_skip):
    hp = lax.Precision.HIGHEST
    lc = SSM_CHUNK
    lr, li = lam_re.astype(F32), lam_im.astype(F32)
    dt = jnp.exp(log_dt.astype(F32))[:, None]
    mag = jnp.exp(lr * dt)
    abar_re, abar_im = mag * jnp.cos(li * dt), mag * jnp.sin(li * dt)
    den = lr * lr + li * li
    nr, ni = abar_re - 1.0, abar_im
    coef_re = (nr * lr + ni * li) / den
    coef_im = (ni * lr - nr * li) / den
    br, bi = b_re.astype(F32), b_im.astype(F32)
    bb_re = coef_re[..., None] * br - coef_im[..., None] * bi
    bb_im = coef_re[..., None] * bi + coef_im[..., None] * br
    j = jnp.arange(lc + 1, dtype=F32)[:, None, None]
    pmag = jnp.exp(j * (lr * dt)[None])
    pw_re, pw_im = pmag * jnp.cos(j * (li * dt)[None]), pmag * jnp.sin(j * (li * dt)[None])
    cr, ci = c_re.astype(F32), c_im.astype(F32)
    ca_re = cr[None] * pw_re[:, :, None, :] - ci[None] * pw_im[:, :, None, :]
    ca_im = cr[None] * pw_im[:, :, None, :] + ci[None] * pw_re[:, :, None, :]
    kern = (jnp.einsum('jgcp,gpd->jgcd', ca_re[:lc], bb_re, precision=hp)
            - jnp.einsum('jgcp,gpd->jgcd', ca_im[:lc], bb_im, precision=hp))
    t_idx = jnp.arange(lc)
    lag = t_idx[None, :] - t_idx[:, None]
    toep = jnp.where((lag >= 0)[:, :, None, None, None], kern[jnp.clip(lag, 0, lc - 1)], 0.0)
    toep = jnp.transpose(toep, (2, 0, 4, 1, 3)).reshape(SSM_GROUPS, lc * SSM_GROUP_CH, lc * SSM_GROUP_CH)
    rev_re, rev_im = pw_re[lc - 1::-1][:lc], pw_im[lc - 1::-1][:lc]
    ts_re = rev_re[:, :, None, :] * jnp.swapaxes(bb_re, 1, 2)[None] - rev_im[:, :, None, :] * jnp.swapaxes(bb_im, 1, 2)[None]
    ts_im = rev_re[:, :, None, :] * jnp.swapaxes(bb_im, 1, 2)[None] + rev_im[:, :, None, :] * jnp.swapaxes(bb_re, 1, 2)[None]
    to_state = jnp.concatenate([ts_re, ts_im, ts_im, ts_re], axis=-1)
    to_state = jnp.transpose(to_state, (1, 0, 2, 3)).reshape(SSM_GROUPS, lc * SSM_GROUP_CH, 4 * SSM_STATE)
    fs = jnp.concatenate([ca_re[1:], -ca_im[1:]], axis=-1)
    from_state = jnp.transpose(fs, (1, 3, 0, 2)).reshape(SSM_GROUPS, 2 * SSM_STATE, lc * SSM_GROUP_CH)
    a_lc_re, a_lc_im = pw_re[lc], pw_im[lc]
    dec_a = jnp.concatenate([a_lc_re, a_lc_re], axis=-1)[:, None, :]
    dec_b = jnp.concatenate([-a_lc_im, a_lc_im], axis=-1)[:, None, :]
    skip = jnp.tile(d_skip.astype(F32), (1, lc))[:, None, :]
    return toep.astype(BF16), to_state.astype(BF16), from_state.astype(BF16), dec_a, dec_b, skip


def _s5_kernel(u_ref, toep_ref, ts_ref, fs_ref, da_ref, db_ref, skip_ref, y_ref, locv_s, locw_s, cin_s, *,
               batch, chunks):
    u = u_ref[...]
    ub = u.astype(BF16)
    y = jnp.dot(ub, toep_ref[...], preferred_element_type=F32) + u * skip_ref[...]
    width = 2 * SSM_STATE
    locv_s[...] = jnp.dot(ub, ts_ref[:, :width], preferred_element_type=F32)
    locw_s[...] = jnp.dot(ub, ts_ref[:, width:], preferred_element_type=F32)
    da = da_ref[...]
    db = db_ref[...]

    def step(n, carry):
        v, w = carry
        rows = pl.ds(n, batch, stride=chunks)
        cin_s[rows, :] = v
        return da * v + db * w + locv_s[rows, :], da * w - db * v + locw_s[rows, :]

    zero = jnp.zeros((batch, width), F32)
    lax.fori_loop(0, chunks, step, (zero, zero))
    y_ref[...] = y + jnp.dot(cin_s[...].astype(BF16), fs_ref[...], preferred_element_type=F32)


def _s5_scan(u, ops, batch, seq):
    toep, to_state, from_state, dec_a, dec_b, skip = ops
    lc, g, ch = SSM_CHUNK, SSM_GROUPS, SSM_GROUP_CH
    chunks = seq // lc
    rows = batch * chunks
    ug = jnp.transpose(u.reshape(rows, lc, g, ch), (2, 0, 1, 3)).reshape(g, rows, lc * ch)
    grp = lambda a: pl.BlockSpec((None,) + a.shape[1:], lambda gi: (gi, 0, 0))
    yg = pl.pallas_call(
        functools.partial(_s5_kernel, batch=batch, chunks=chunks),
        grid=(g,),
        in_specs=[grp(ug), grp(toep), grp(to_state), grp(from_state), grp(dec_a), grp(dec_b), grp(skip)],
        out_specs=grp(ug),
        out_shape=jax.ShapeDtypeStruct(ug.shape, F32),
        scratch_shapes=[pltpu.VMEM((rows, 2 * SSM_STATE), F32)] * 3,
        compiler_params=_cparams(("parallel",)),
        name="s5_scan",
    )(ug, toep, to_state, from_state, dec_a, dec_b, skip)
    return jnp.transpose(yg.reshape(g, rows, lc, ch), (1, 2, 0, 3)).reshape(batch * seq, g * ch)


def _odd_mix_kernel(ys_ref, gb_ref, gc_ref, xt_ref, gch_ref, xth_ref, wglu_ref, cw_ref, yc_ref, yd_ref, *, tiles_per_seq):
    y = _gelu_tanh(ys_ref[...])
    yc_ref[...] = (y * _sigmoid(jnp.dot(y.astype(BF16), wglu_ref[...], preferred_element_type=F32))).astype(yc_ref.dtype)
    z = gc_ref[...] * xt_ref[...]
    seq_start = (pl.program_id(0) % tiles_per_seq) == 0
    zh = jnp.where(seq_start, 0.0, gch_ref[...] * xth_ref[...])
    row = lax.broadcasted_iota(jnp.int32, z.shape, 0)
    z1 = jnp.where(row == 0, zh[7:8, :], pltpu.roll(z, 1, 0))
    z2 = jnp.where(row == 0, zh[6:7, :], jnp.where(row == 1, zh[7:8, :], pltpu.roll(z, 2, 0)))
    cw = cw_ref[...]
    conv = cw[0:1, :] * z + cw[1:2, :] * z1 + cw[2:3, :] * z2
    yd_ref[...] = (gb_ref[...] * conv).astype(yd_ref.dtype)


def _odd_mix(y_ssm, proj, w_glu, conv_w, seq, tm):
    t = y_ssm.shape[0]
    w = CONV_WIDTH
    halo = 8
    col = lambda c: pl.BlockSpec((tm, w), lambda i: (i, c))
    prev = lambda c: pl.BlockSpec((halo, w), lambda i: (jnp.maximum(i * (tm // halo) - 1, 0), c))
    return pl.pallas_call(
        functools.partial(_odd_mix_kernel, tiles_per_seq=seq // tm),
        grid=(t // tm,),
        in_specs=[col(0), col(0), col(1), col(2), prev(1), prev(2), _resident(w_glu.shape), _resident(conv_w.shape)],
        out_specs=[col(0), col(0)],
        out_shape=[jax.ShapeDtypeStruct((t, SSM_WIDTH), BF16), jax.ShapeDtypeStruct((t, w), BF16)],
        compiler_params=_cparams(("parallel",)),
        name="odd_mix",
    )(y_ssm, proj, proj, proj, proj, proj, w_glu, conv_w)


def kernel(x, p, norm_mix, norm_mlp, norm_ple, w_mlp_in, w_mlp_out, w_ple_proj, w_ple_gate, attn_w_in, attn_w_out, diff_lq1, diff_lk1, diff_lq2, diff_lk2, diff_sub_gain, rc_w_in, rc_w_out, ssm_lambda_re, ssm_lambda_im, ssm_log_dt, ssm_b_re, ssm_b_im, ssm_c_re, ssm_c_im, ssm_d, ssm_w_glu, conv_w, norm_final):
    b, s, d = x.shape
    depth = p.shape[0]
    t = b * s
    tm = 512
    assert s % DIL_TILE == 0 and s % tm == 0 and d == 2 * DIFF_WIDTH
    h = x.reshape(t, d)
    for i in range(depth):
        if i % 2 == 0:
            e = i // 2
            lam_init = 0.8 - 0.6 * math.exp(-0.3 * i)
            pa, pb = _norm_matmul(h, norm_mix[i], attn_w_in[e].astype(BF16),
                                  ((0, 3 * DIFF_WIDTH), (3 * DIFF_WIDTH, 3 * DIFF_WIDTH + 3 * SWA_WIDTH)),
                                  (BF16, F32), tm)
            y1 = _diff_attn(pa.reshape(b, s, -1), diff_lq1[e], diff_lk1[e], diff_lq2[e], diff_lk2[e],
                            diff_sub_gain[e], lam_init, tq=256).reshape(t, -1)
            y2 = _dil_attn(pb.reshape(b, s, -1)).reshape(t, -1)
            wo = attn_w_out[e]
        else:
            o = i // 2
            u, pr = _norm_matmul(h, norm_mix[i], rc_w_in[o].astype(BF16),
                                 ((0, SSM_WIDTH), (SSM_WIDTH, SSM_WIDTH + 3 * CONV_WIDTH)), (F32, F32), tm)
            ops = _s5_operators(ssm_lambda_re[o], ssm_lambda_im[o], ssm_log_dt[o], ssm_b_re[o], ssm_b_im[o],
                                ssm_c_re[o], ssm_c_im[o], ssm_d[o])
            y_ssm = _s5_scan(u, ops, b, s)
            y1, y2 = _odd_mix(y_ssm, pr, ssm_w_glu[o].astype(BF16), conv_w[o].astype(F32), s, tm)
            wo = rc_w_out[o]
        h = _post(h, y1, y2, wo.astype(BF16), norm_mlp[i], w_mlp_in[i].astype(BF16), w_mlp_out[i].astype(BF16),
                  norm_ple[i], p[i].reshape(t, -1), w_ple_proj[i].astype(BF16), w_ple_gate[i].astype(BF16),
                  norm_final, i == depth - 1, tm)
    return h.reshape(b, s, d)
```

```python
import functools
import math

import jax
import jax.numpy as jnp
from jax import lax
from jax.experimental import pallas as pl
from jax.experimental.pallas import tpu as pltpu

F32 = jnp.float32
BF16 = jnp.bfloat16

EPS = 1e-6
PLE_DIM = 256
DIFF_DH = 64
DIFF_HEADS = 4
DIFF_WIDTH = 512
SWA_DH = 64
SWA_WIDTH = 512
DILATIONS = (16, 4, 1)
N_BACK = 128
DIL_TILE = N_BACK * 16
DIL_GROUP = 4
SSM_GROUP_CH = 16
SSM_STATE = 64
SSM_WIDTH = 512
SSM_GROUPS = 32
SSM_BUNDLES = 4
SSM_CHUNK = 16
SSM_TILE = 1024
CONV_WIDTH = 512
LANES = 128
NEG_BIG = -1e30
VMEM_LIMIT = 60 * 1024 * 1024


def _cparams(sem):
    return pltpu.CompilerParams(dimension_semantics=sem, vmem_limit_bytes=VMEM_LIMIT)


def _resident(shape):
    nd = len(shape)
    return pl.BlockSpec(shape, lambda *_: (0,) * nd, pipeline_mode=pl.Buffered(1))


def _rms(x, g, eps=EPS):
    return x * lax.rsqrt(jnp.mean(x * x, axis=-1, keepdims=True) + eps) * g


def _sigmoid(x):
    return 1.0 / (1.0 + jnp.exp(-x))


def _gelu_tanh(x):
    c = math.sqrt(2.0 / math.pi)
    return x * (0.5 * (1.0 + jnp.tanh(c * (x + 0.044715 * (x * x * x)))))


def _norm_matmul_kernel(h_ref, g_ref, w_ref, *o_refs, splits):
    xn = _rms(h_ref[...], g_ref[...]).astype(BF16)
    for o_ref, (c0, c1) in zip(o_refs, splits):
        o_ref[...] = jnp.dot(xn, w_ref[:, c0:c1], preferred_element_type=F32).astype(o_ref.dtype)


def _norm_matmul(h, g, w, splits, dtypes, tm):
    t, d = h.shape
    n = w.shape[1]
    return pl.pallas_call(
        functools.partial(_norm_matmul_kernel, splits=splits),
        grid=(t // tm,),
        in_specs=[pl.BlockSpec((tm, d), lambda i: (i, 0)), _resident((1, d)), _resident((d, n))],
        out_specs=[pl.BlockSpec((tm, c1 - c0), lambda i: (i, 0)) for (c0, c1) in splits],
        out_shape=[jax.ShapeDtypeStruct((t, c1 - c0), dt) for (c0, c1), dt in zip(splits, dtypes)],
        compiler_params=_cparams(("parallel",)),
        name="norm_matmul",
    )(h, g.reshape(1, d), w)


def _post_kernel(h_ref, y1_ref, y2_ref, wo_ref, gm_ref, w1_ref, w2_ref, gp_ref, p_ref, wp_ref, wg_ref,
                 gf_ref, o_ref, *, ff_chunk, final):
    k1 = y1_ref.shape[1]
    h = h_ref[...]
    h = h + jnp.dot(y1_ref[...], wo_ref[:k1, :], preferred_element_type=F32)
    h = h + jnp.dot(y2_ref[...], wo_ref[k1:, :], preferred_element_type=F32)
    hn = _rms(h, gm_ref[...]).astype(BF16)
    d_ff = w1_ref.shape[1]
    acc = jnp.zeros_like(h)
    for c in range(d_ff // ff_chunk):
        a = jnp.dot(hn, w1_ref[:, c * ff_chunk:(c + 1) * ff_chunk], preferred_element_type=F32)
        a = jnp.square(jnp.maximum(a, 0.0)).astype(BF16)
        acc = acc + jnp.dot(a, w2_ref[c * ff_chunk:(c + 1) * ff_chunk, :], preferred_element_type=F32)
    h = h + acc
    hn = _rms(h, gp_ref[...]).astype(BF16)
    gate = _sigmoid(jnp.dot(hn, wg_ref[...], preferred_element_type=F32))
    emb = jnp.dot(p_ref[...].astype(BF16), wp_ref[...], preferred_element_type=F32)
    h = h + emb * gate
    if final:
        h = _rms(h, gf_ref[...])
    o_ref[...] = h


def _post(h, y1, y2, wo, gm, w1, w2, gp, p, wp, wg, gf, final, tm):
    t, d = h.shape
    row = lambda width: pl.BlockSpec((tm, width), lambda i: (i, 0))
    return pl.pallas_call(
        functools.partial(_post_kernel, ff_chunk=1024, final=final),
        grid=(t // tm,),
        in_specs=[row(d), row(y1.shape[1]), row(y2.shape[1]), _resident(wo.shape), _resident((1, d)),
                  _resident(w1.shape), _resident(w2.shape), _resident((1, d)), row(p.shape[1]),
                  _resident(wp.shape), _resident(wg.shape), _resident((1, d))],
        out_specs=row(d),
        out_shape=jax.ShapeDtypeStruct((t, d), F32),
        compiler_params=_cparams(("parallel",)),
        name="post_mixer",
    )(h, y1, y2, wo, gm.reshape(1, d), w1, w2, gp.reshape(1, d), p, wp, wg, gf.reshape(1, d))


def _stack_heads(x, scale):
    lane = lax.broadcasted_iota(jnp.int32, x.shape, 1)
    zero = jnp.zeros_like(x)
    lo = jnp.where(lane < 64, x, zero)
    hi = jnp.where(lane >= 64, x, zero)
    return jnp.concatenate([lo, hi], axis=0) * scale


def _diff_attn_kernel(q_ref, k_ref, v_ref, lq1_ref, lk1_ref, lq2_ref, lk2_ref, sg_ref, bias_ref, o_ref,
                      m_s, l_s, acc_s, *,
                      tq, bk, lam_init):
    qi = pl.program_id(2)
    qs = _stack_heads(q_ref[...], jnp.asarray(DIFF_DH ** -0.5, BF16))
    n_full = (qi * tq) // bk

    def scores(j):
        k = k_ref[pl.ds(pl.multiple_of(j * bk, bk), bk), :]
        return lax.dot_general(qs, k, (((1,), (1,)), ((), ())), preferred_element_type=F32)

    def update(j, s, masked):
        if masked:
            s = s + bias_ref[(qi * tq - j * bk) // tq]
        v = v_ref[pl.ds(pl.multiple_of(j * bk, bk), bk), :]
        m = m_s[...]
        m_new = jnp.maximum(m, jnp.max(s, axis=-1, keepdims=True))
        alpha = jnp.exp(m - m_new)
        p = jnp.exp(s - jnp.concatenate([m_new] * (bk // LANES), axis=1))
        m_s[...] = m_new
        l_s[...] = alpha * l_s[...] + sum(p[:, c * LANES:(c + 1) * LANES] for c in range(bk // LANES))
        acc_s[...] = alpha * acc_s[...] + jnp.dot(p.astype(BF16), v, preferred_element_type=F32)

    m_s[...] = jnp.full(m_s.shape, NEG_BIG, F32)
    l_s[...] = jnp.zeros(l_s.shape, F32)
    acc_s[...] = jnp.zeros(acc_s.shape, F32)

    def body(j, s):
        s_next = scores(j + 1)
        update(j, s, False)
        return s_next

    s = lax.fori_loop(0, n_full, body, scores(0))
    update(n_full, s, True)
    o = acc_s[...] / jnp.sum(l_s[...], axis=-1, keepdims=True)
    lam = (jnp.exp(jnp.sum(lq1_ref[...] * lk1_ref[...], keepdims=True))
           - jnp.exp(jnp.sum(lq2_ref[...] * lk2_ref[...], keepdims=True)) + lam_init)
    o = o[:tq] - lam * o[tq:]
    o = o * lax.rsqrt(jnp.mean(o * o, axis=-1, keepdims=True) + 1e-5) * sg_ref[...]
    o_ref[...] = (o * (1.0 - lam_init)).astype(o_ref.dtype)


def _diff_attn(proj, lq1, lk1, lq2, lk2, sub_gain, lam_init, tq, bk):
    b, s, _ = proj.shape
    vec = lambda a: a.astype(F32).reshape(1, -1)
    off = jnp.arange(bk // tq)[:, None, None] * tq
    qpos = off + (jnp.arange(2 * tq) % tq)[None, :, None]
    bias = jnp.where(jnp.arange(bk)[None, None, :] <= qpos, 0.0, NEG_BIG).astype(F32)
    return pl.pallas_call(
        functools.partial(_diff_attn_kernel, tq=tq, bk=bk, lam_init=lam_init),
        grid=(b, DIFF_HEADS, s // tq),
        in_specs=[pl.BlockSpec((None, tq, LANES), lambda bi, h, qi: (bi, qi, h)),
                  pl.BlockSpec((None, s, LANES), lambda bi, h, qi: (bi, 0, DIFF_HEADS + h)),
                  pl.BlockSpec((None, s, LANES), lambda bi, h, qi: (bi, 0, 2 * DIFF_HEADS + h)),
                  _resident((1, DIFF_DH)), _resident((1, DIFF_DH)), _resident((1, DIFF_DH)),
                  _resident((1, DIFF_DH)), _resident((1, 2 * DIFF_DH)), _resident(bias.shape)],
        out_specs=pl.BlockSpec((None, tq, LANES), lambda bi, h, qi: (bi, qi, h)),
        out_shape=jax.ShapeDtypeStruct((b, s, DIFF_WIDTH), BF16),
        scratch_shapes=[pltpu.VMEM((2 * tq, LANES), F32)] * 3,
        compiler_params=_cparams(("parallel", "parallel", "arbitrary")),
        name="diff_attn",
    )(proj, proj, proj, vec(lq1), vec(lk1), vec(lq2), vec(lk2), vec(sub_gain), bias)


def _dil_attn_kernel(q_ref, kp_ref, kc_ref, vp_ref, vc_ref, o_ref, k_s, v_s, acc_s, m_s, l_s, bias_s):
    first = pl.program_id(2) == 0
    tile = q_ref.shape[0]
    k_s[pl.ds(0, tile), :] = kp_ref[...]
    k_s[pl.ds(tile, tile), :] = kc_ref[...]
    v_s[pl.ds(0, tile), :] = vp_ref[...]
    v_s[pl.ds(tile, tile), :] = vc_ref[...]
    scale = jnp.asarray(SWA_DH ** -0.5, BF16)
    lane = lax.broadcasted_iota(jnp.int32, (N_BACK, LANES), 1)
    lo = lane < 64
    r = lax.broadcasted_iota(jnp.int32, (2 * N_BACK, 2 * N_BACK), 0)
    qidx = jnp.where(r >= N_BACK, r - N_BACK, r)
    kk = lax.broadcasted_iota(jnp.int32, (2 * N_BACK, 2 * N_BACK), 1)
    band = (kk >= qidx) & (kk <= qidx + N_BACK)
    bias_s[0] = jnp.where(band, 0.0, NEG_BIG)
    bias_s[1] = jnp.where(band & jnp.logical_or(jnp.logical_not(first), kk >= N_BACK), 0.0, NEG_BIG)

    n_sub = DIL_TILE // N_BACK
    for dil in DILATIONS:

        def windows(n, dil=dil):
            st = (n % dil) + (n // dil) * (N_BACK * dil)
            if dil == 1 and not isinstance(n, int):
                st = pl.multiple_of(st, N_BACK)
            kbase = tile + st - N_BACK * dil
            if dil > 1:
                return pl.ds(st, N_BACK, stride=dil), pl.ds(kbase, 2 * N_BACK, stride=dil), kbase
            return pl.ds(st, N_BACK), pl.ds(kbase, 2 * N_BACK), kbase

        def scores(n):
            rows, krows, _ = windows(n)
            qs = _stack_heads(q_ref[rows, :].astype(BF16), scale)
            kb = k_s[krows, :].astype(BF16)
            return lax.dot_general(qs, kb, (((1,), (1,)), ((), ())), preferred_element_type=F32)

        def finish(n, s, residue_head, dil=dil):
            rows, krows, _ = windows(n)
            vb = v_s[krows, :].astype(BF16)
            s = s + bias_s[1 if residue_head else 0]
            m2 = jnp.max(s, axis=-1, keepdims=True)
            p = jnp.exp(s - m2)
            l2 = jnp.sum(p, axis=-1, keepdims=True)
            pv = jnp.dot(p.astype(BF16), vb, preferred_element_type=F32)
            acc_c = jnp.where(lo, pv[:N_BACK], pv[N_BACK:])
            m_c = jnp.where(lo, m2[:N_BACK], m2[N_BACK:])
            l_c = jnp.where(lo, l2[:N_BACK], l2[N_BACK:])
            if dil == DILATIONS[0]:
                acc_s[rows, :] = acc_c
                m_s[rows, :] = m_c
                l_s[rows, :] = l_c
            else:
                m_o = m_s[rows, :]
                m_n = jnp.maximum(m_o, m_c)
                a_o = jnp.exp(m_o - m_n)
                a_c = jnp.exp(m_c - m_n)
                acc_n = a_o * acc_s[rows, :] + a_c * acc_c
                l_n = a_o * l_s[rows, :] + a_c * l_c
                if dil == DILATIONS[-1]:
                    o_ref[rows, :] = (acc_n / l_n).astype(o_ref.dtype)
                else:
                    acc_s[rows, :] = acc_n
                    m_s[rows, :] = m_n
                    l_s[rows, :] = l_n

        n_grp = n_sub // DIL_GROUP
        head_flags = [tuple(g * DIL_GROUP + e < dil for e in range(DIL_GROUP)) for g in range(n_grp)]

        def score_group(g, scores=scores):
            return tuple(scores(g * DIL_GROUP + e) for e in range(DIL_GROUP))

        def finish_group(g, ss, flags, finish=finish):
            for e in range(DIL_GROUP):
                finish(g * DIL_GROUP + e, ss[e], flags[e])

        def body(g, ss, flags, score_group=score_group, finish_group=finish_group):
            nxt = score_group(g + 1)
            finish_group(g, ss, flags)
            return nxt

        ss = score_group(0)
        g0 = 0
        while g0 < n_grp - 1:
            g1 = g0 + 1
            while g1 < n_grp - 1 and head_flags[g1] == head_flags[g0]:
                g1 += 1
            ss = lax.fori_loop(g0, g1, functools.partial(body, flags=head_flags[g0]), ss)
            g0 = g1
        finish_group(n_grp - 1, ss, head_flags[-1])


def _dil_attn(proj):
    b, s, _ = proj.shape
    pairs = SWA_WIDTH // LANES
    cur = lambda part: pl.BlockSpec((None, DIL_TILE, LANES), lambda bi, hp, i: (bi, i, part * pairs + hp))
    prev = lambda part: pl.BlockSpec((None, DIL_TILE, LANES),
                                     lambda bi, hp, i: (bi, jnp.maximum(i - 1, 0), part * pairs + hp))
    return pl.pallas_call(
        _dil_attn_kernel,
        grid=(b, pairs, s // DIL_TILE),
        in_specs=[cur(0), prev(1), cur(1), prev(2), cur(2)],
        out_specs=pl.BlockSpec((None, DIL_TILE, LANES), lambda bi, hp, i: (bi, i, hp)),
        out_shape=jax.ShapeDtypeStruct((b, s, SWA_WIDTH), BF16),
        scratch_shapes=[pltpu.VMEM((2 * DIL_TILE, LANES), F32), pltpu.VMEM((2 * DIL_TILE, LANES), F32),
                        pltpu.VMEM((DIL_TILE, LANES), F32), pltpu.VMEM((DIL_TILE, LANES), F32),
                        pltpu.VMEM((DIL_TILE, LANES), F32), pltpu.VMEM((2, 2 * N_BACK, 2 * N_BACK), F32)],
        compiler_params=_cparams(("parallel", "parallel", "arbitrary")),
        name="dilated_attn",
    )(proj, proj, proj, proj, proj)


def _spread_groups(x):
    gb = x.shape[2]
    out = jnp.zeros(x.shape[:5] + (gb,) + x.shape[5:], x.dtype)
    for g in range(gb):
        out = out.at[:, :, g, :, :, g, :].set(x[:, :, g])
    return out


def _s5_operators(lam_re, lam_im, log_dt, b_re, b_im, c_re, c_im, d_skip):
    hp = lax.Precision.HIGHEST
    lc = SSM_CHUNK
    lr, li = lam_re.astype(F32), lam_im.astype(F32)
    dt = jnp.exp(log_dt.astype(F32))[:, None]
    mag = jnp.exp(lr * dt)
    abar_re, abar_im = mag * jnp.cos(li * dt), mag * jnp.sin(li * dt)
    den = lr * lr + li * li
    nr, ni = abar_re - 1.0, abar_im
    coef_re = (nr * lr + ni * li) / den
    coef_im = (ni * lr - nr * li) / den
    br, bi = b_re.astype(F32), b_im.astype(F32)
    bb_re = coef_re[..., None] * br - coef_im[..., None] * bi
    bb_im = coef_re[..., None] * bi + coef_im[..., None] * br
    j = jnp.arange(lc + 1, dtype=F32)[:, None, None]
    pmag = jnp.exp(j * (lr * dt)[None])
    pw_re, pw_im = pmag * jnp.cos(j * (li * dt)[None]), pmag * jnp.sin(j * (li * dt)[None])
    cr, ci = c_re.astype(F32), c_im.astype(F32)
    ca_re = cr[None] * pw_re[:, :, None, :] - ci[None] * pw_im[:, :, None, :]
    ca_im = cr[None] * pw_im[:, :, None, :] + ci[None] * pw_re[:, :, None, :]
    kern = (jnp.einsum('jgcp,gpd->jgcd', ca_re[:lc], bb_re, precision=hp)
            - jnp.einsum('jgcp,gpd->jgcd', ca_im[:lc], bb_im, precision=hp))
    nb, gb, ch, st = SSM_BUNDLES, SSM_GROUPS // SSM_BUNDLES, SSM_GROUP_CH, SSM_STATE
    t_idx = jnp.arange(lc)
    lag = t_idx[None, :] - t_idx[:, None]
    toep = jnp.where((lag >= 0)[:, :, None, None, None], kern[jnp.clip(lag, 0, lc - 1)], 0.0)
    toep = jnp.transpose(toep, (2, 0, 4, 1, 3)).reshape(nb, gb, lc, ch, lc, ch)
    toep = _spread_groups(jnp.transpose(toep, (0, 2, 1, 3, 4, 5)))
    toep = toep.reshape(nb, lc * LANES, lc * LANES)
    rev_re, rev_im = pw_re[lc - 1::-1][:lc], pw_im[lc - 1::-1][:lc]
    ts_re = rev_re[:, :, None, :] * jnp.swapaxes(bb_re, 1, 2)[None] - rev_im[:, :, None, :] * jnp.swapaxes(bb_im, 1, 2)[None]
    ts_im = rev_re[:, :, None, :] * jnp.swapaxes(bb_im, 1, 2)[None] + rev_im[:, :, None, :] * jnp.swapaxes(bb_re, 1, 2)[None]
    ts = jnp.stack([ts_re, ts_im]).reshape(2, lc, nb, gb, ch, st)
    to_state = _spread_groups(jnp.transpose(ts, (2, 1, 3, 4, 0, 5)))
    to_state = to_state.reshape(nb, lc * LANES, 2 * gb * st)
    fs = jnp.stack([ca_re[1:], -ca_im[1:]]).reshape(2, lc, nb, gb, ch, st)
    from_state = _spread_groups(jnp.transpose(fs, (2, 0, 3, 5, 1, 4)))
    from_state = from_state.reshape(nb, 2 * gb * st, lc * LANES)
    a_lc_re, a_lc_im = pw_re[lc].reshape(nb, gb * st), pw_im[lc].reshape(nb, gb * st)
    dec_a = jnp.concatenate([a_lc_re, a_lc_re], axis=-1)[:, None, :]
    dec_b = jnp.concatenate([-a_lc_im, a_lc_im], axis=-1)[:, None, :]
    skip = d_skip.astype(F32).reshape(nb, 1, LANES)
    return toep.astype(BF16), to_state.astype(BF16), from_state.astype(BF16), dec_a, dec_b, skip


def _s5_kernel(u_ref, toep_ref, ts_ref, fs_ref, da_ref, db_ref, skip_ref, y_ref, lhs_s, ug_s, loc_s, cin_s,
               state_s):
    lc = SSM_CHUNK
    batch, tile, _ = u_ref.shape
    chunks = tile // lc
    slabs = state_s.shape[0]

    for b in range(batch):
        for s in range(lc):
            piece = u_ref[b, pl.ds(s, chunks, stride=lc), :]
            ug_s[s, pl.ds(b * chunks, chunks), :] = piece
            lhs_s[pl.ds(b * chunks, chunks), pl.ds(s * LANES, LANES)] = piece.astype(BF16)
    lhs = lhs_s[...]
    y = jnp.dot(lhs, toep_ref[...], preferred_element_type=F32)
    loc = jnp.dot(lhs, ts_ref[...], preferred_element_type=F32)
    for k in range(slabs):
        loc_s[k] = loc[:, k * LANES:(k + 1) * LANES]

    @pl.when(pl.program_id(1) == 0)
    def _():
        state_s[...] = jnp.zeros(state_s.shape, F32)

    da = [da_ref[:, k * LANES:(k + 1) * LANES] for k in range(slabs)]
    db = [db_ref[:, k * LANES:(k + 1) * LANES] for k in range(slabs)]

    def step(n, x):
        rows = pl.ds(n, batch, stride=chunks)
        for k in range(slabs):
            cin_s[k, rows, :] = x[k]
        return tuple(da[k] * x[k] + db[k] * x[(k + slabs // 2) % slabs] + loc_s[k, rows, :] for k in range(slabs))

    x = lax.fori_loop(0, chunks, step, tuple(state_s[k] for k in range(slabs)))
    for k in range(slabs):
        state_s[k] = x[k]
    cin = jnp.concatenate([cin_s[k] for k in range(slabs)], axis=1).astype(BF16)
    y = y + jnp.dot(cin, fs_ref[...], preferred_element_type=F32)
    skip = skip_ref[...]
    for b in range(batch):
        for t in range(lc):
            rows = pl.ds(b * chunks, chunks)
            y_ref[b, pl.ds(t, chunks, stride=lc), :] = (y[b * chunks:(b + 1) * chunks, t * LANES:(t + 1) * LANES]
                                                        + ug_s[t, rows, :] * skip)


def _s5_scan(u, ops, tile):
    toep, to_state, from_state, dec_a, dec_b, skip = ops
    batch, seq, _ = u.shape
    lc = SSM_CHUNK
    rows = batch * (tile // lc)
    slabs = to_state.shape[2] // LANES
    act = pl.BlockSpec((batch, tile, LANES), lambda bi, si: (0, si, bi))
    per_bundle = lambda a, **kw: pl.BlockSpec((None,) + a.shape[1:], lambda bi, si: (bi, 0, 0), **kw)
    once = dict(pipeline_mode=pl.Buffered(1))
    return pl.pallas_call(
        _s5_kernel,
        grid=(SSM_BUNDLES, seq // tile),
        in_specs=[act, per_bundle(toep, **once), per_bundle(to_state, **once), per_bundle(from_state, **once),
                  per_bundle(dec_a), per_bundle(dec_b), per_bundle(skip)],
        out_specs=act,
        out_shape=jax.ShapeDtypeStruct(u.shape, F32),
        scratch_shapes=[pltpu.VMEM((rows, lc * LANES), BF16), pltpu.VMEM((lc, rows, LANES), F32),
                        pltpu.VMEM((slabs, rows, LANES), F32), pltpu.VMEM((slabs, rows, LANES), F32),
                        pltpu.VMEM((slabs, batch, LANES), F32)],
        compiler_params=_cparams(("parallel", "arbitrary")),
        name="s5_scan",
    )(u, toep, to_state, from_state, dec_a, dec_b, skip)


def _odd_mix_kernel(ys_ref, gb_ref, gc_ref, xt_ref, gch_ref, xth_ref, wglu_ref, cw_ref, yc_ref, yd_ref, *, tiles_per_seq):
    y = _gelu_tanh(ys_ref[...])
    yc_ref[...] = (y * _sigmoid(jnp.dot(y.astype(BF16), wglu_ref[...], preferred_element_type=F32))).astype(yc_ref.dtype)
    z = gc_ref[...] * xt_ref[...]
    seq_start = (pl.program_id(0) % tiles_per_seq) == 0
    zh = jnp.where(seq_start, 0.0, gch_ref[...] * xth_ref[...])
    row = lax.broadcasted_iota(jnp.int32, z.shape, 0)
    z1 = jnp.where(row == 0, zh[7:8, :], pltpu.roll(z, 1, 0))
    z2 = jnp.where(row == 0, zh[6:7, :], jnp.where(row == 1, zh[7:8, :], pltpu.roll(z, 2, 0)))
    cw = cw_ref[...]
    conv = cw[0:1, :] * z + cw[1:2, :] * z1 + cw[2:3, :] * z2
    yd_ref[...] = (gb_ref[...] * conv).astype(yd_ref.dtype)


def _odd_mix(y_ssm, proj, w_glu, conv_w, seq, tm):
    t = y_ssm.shape[0]
    w = CONV_WIDTH
    halo = 8
    col = lambda c: pl.BlockSpec((tm, w), lambda i: (i, c))
    prev = lambda c: pl.BlockSpec((halo, w), lambda i: (jnp.maximum(i * (tm // halo) - 1, 0), c))
    return pl.pallas_call(
        functools.partial(_odd_mix_kernel, tiles_per_seq=seq // tm),
        grid=(t // tm,),
        in_specs=[col(0), col(0), col(1), col(2), prev(1), prev(2), _resident(w_glu.shape), _resident(conv_w.shape)],
        out_specs=[col(0), col(0)],
        out_shape=[jax.ShapeDtypeStruct((t, SSM_WIDTH), BF16), jax.ShapeDtypeStruct((t, w), BF16)],
        compiler_params=_cparams(("parallel",)),
        name="odd_mix",
    )(y_ssm, proj, proj, proj, proj, proj, w_glu, conv_w)


def kernel(x, p, norm_mix, norm_mlp, norm_ple, w_mlp_in, w_mlp_out, w_ple_proj, w_ple_gate, attn_w_in, attn_w_out, diff_lq1, diff_lk1, diff_lq2, diff_lk2, diff_sub_gain, rc_w_in, rc_w_out, ssm_lambda_re, ssm_lambda_im, ssm_log_dt, ssm_b_re, ssm_b_im, ssm_c_re, ssm_c_im, ssm_d, ssm_w_glu, conv_w, norm_final):
    b, s, d = x.shape
    depth = p.shape[0]
    t = b * s
    tm = 512
    assert s % DIL_TILE == 0 and s % tm == 0 and d == 2 * DIFF_WIDTH
    h = x.reshape(t, d)
    for i in range(depth):
        if i % 2 == 0:
            e = i // 2
            lam_init = 0.8 - 0.6 * math.exp(-0.3 * i)
            pa, pb = _norm_matmul(h, norm_mix[i], attn_w_in[e].astype(BF16),
                                  ((0, 3 * DIFF_WIDTH), (3 * DIFF_WIDTH, 3 * DIFF_WIDTH + 3 * SWA_WIDTH)),
                                  (BF16, F32), tm)
            y1 = _diff_attn(pa.reshape(b, s, -1), diff_lq1[e], diff_lk1[e], diff_lq2[e], diff_lk2[e],
                            diff_sub_gain[e], lam_init, tq=256, bk=512).reshape(t, -1)
            y2 = _dil_attn(pb.reshape(b, s, -1)).reshape(t, -1)
            wo = attn_w_out[e]
        else:
            o = i // 2
            u, pr = _norm_matmul(h, norm_mix[i], rc_w_in[o].astype(BF16),
                                 ((0, SSM_WIDTH), (SSM_WIDTH, SSM_WIDTH + 3 * CONV_WIDTH)), (F32, F32), tm)
            ops = _s5_operators(ssm_lambda_re[o], ssm_lambda_im[o], ssm_log_dt[o], ssm_b_re[o], ssm_b_im[o],
                                ssm_c_re[o], ssm_c_im[o], ssm_d[o])
            y_ssm = _s5_scan(u.reshape(b, s, -1), ops, min(SSM_TILE, s)).reshape(t, -1)
            y1, y2 = _odd_mix(y_ssm, pr, ssm_w_glu[o].astype(BF16), conv_w[o].astype(F32), s, tm)
            wo = rc_w_out[o]
        h = _post(h, y1, y2, wo.astype(BF16), norm_mlp[i], w_mlp_in[i].astype(BF16), w_mlp_out[i].astype(BF16),
                  norm_ple[i], p[i].reshape(t, -1), w_ple_proj[i].astype(BF16), w_ple_gate[i].astype(BF16),
                  norm_final, i == depth - 1, tm)
    return h.reshape(b, s, d)
```

```python
import functools
import math

import jax
import jax.numpy as jnp
from jax import lax
from jax.experimental import pallas as pl
from jax.experimental.pallas import tpu as pltpu

F32 = jnp.float32
BF16 = jnp.bfloat16

EPS = 1e-6
PLE_DIM = 256
DIFF_DH = 64
DIFF_HEADS = 4
DIFF_WIDTH = 512
SWA_DH = 64
SWA_WIDTH = 512
DILATIONS = (16, 4, 1)
N_BACK = 128
DIL_TILE = N_BACK * 16
DIL_GROUP = 4
SSM_GROUP_CH = 16
SSM_STATE = 64
SSM_WIDTH = 512
SSM_GROUPS = 32
SSM_BUNDLES = 4
SSM_CHUNK = 16
SSM_TILE = 1024
CONV_WIDTH = 512
LANES = 128
NEG_BIG = -1e30
VMEM_LIMIT = 60 * 1024 * 1024


def _cparams(sem):
    return pltpu.CompilerParams(dimension_semantics=sem, vmem_limit_bytes=VMEM_LIMIT)


def _resident(shape):
    nd = len(shape)
    return pl.BlockSpec(shape, lambda *_: (0,) * nd, pipeline_mode=pl.Buffered(1))


def _rms(x, g, eps=EPS):
    return x * lax.rsqrt(jnp.mean(x * x, axis=-1, keepdims=True) + eps) * g


def _sigmoid(x):
    return 1.0 / (1.0 + jnp.exp(-x))


def _gelu_tanh(x):
    c = math.sqrt(2.0 / math.pi)
    return x * (0.5 * (1.0 + jnp.tanh(c * (x + 0.044715 * (x * x * x)))))


def _norm_matmul_kernel(h_ref, g_ref, w_ref, *o_refs, splits):
    xn = _rms(h_ref[...], g_ref[...]).astype(BF16)
    for o_ref, (c0, c1) in zip(o_refs, splits):
        o_ref[...] = jnp.dot(xn, w_ref[:, c0:c1], preferred_element_type=F32).astype(o_ref.dtype)


def _norm_matmul(h, g, w, splits, dtypes, tm):
    t, d = h.shape
    n = w.shape[1]
    return pl.pallas_call(
        functools.partial(_norm_matmul_kernel, splits=splits),
        grid=(t // tm,),
        in_specs=[pl.BlockSpec((tm, d), lambda i: (i, 0)), _resident((1, d)), _resident((d, n))],
        out_specs=[pl.BlockSpec((tm, c1 - c0), lambda i: (i, 0)) for (c0, c1) in splits],
        out_shape=[jax.ShapeDtypeStruct((t, c1 - c0), dt) for (c0, c1), dt in zip(splits, dtypes)],
        compiler_params=_cparams(("parallel",)),
        name="norm_matmul",
    )(h, g.reshape(1, d), w)


def _post_kernel(h_ref, y1_ref, y2_ref, wo_ref, gm_ref, w1_ref, w2_ref, gp_ref, p_ref, wp_ref, wg_ref,
                 gf_ref, o_ref, *, ff_chunk, final):
    k1 = y1_ref.shape[1]
    h = h_ref[...]
    h = h + jnp.dot(y1_ref[...], wo_ref[:k1, :], preferred_element_type=F32)
    h = h + jnp.dot(y2_ref[...], wo_ref[k1:, :], preferred_element_type=F32)
    hn = _rms(h, gm_ref[...]).astype(BF16)
    d_ff = w1_ref.shape[1]
    acc = jnp.zeros_like(h)
    for c in range(d_ff // ff_chunk):
        a = jnp.dot(hn, w1_ref[:, c * ff_chunk:(c + 1) * ff_chunk], preferred_element_type=F32)
        a = jnp.square(jnp.maximum(a, 0.0)).astype(BF16)
        acc = acc + jnp.dot(a, w2_ref[c * ff_chunk:(c + 1) * ff_chunk, :], preferred_element_type=F32)
    h = h + acc
    hn = _rms(h, gp_ref[...]).astype(BF16)
    gate = _sigmoid(jnp.dot(hn, wg_ref[...], preferred_element_type=F32))
    emb = jnp.dot(p_ref[...].astype(BF16), wp_ref[...], preferred_element_type=F32)
    h = h + emb * gate
    if final:
        h = _rms(h, gf_ref[...])
    o_ref[...] = h


def _post(h, y1, y2, wo, gm, w1, w2, gp, p, wp, wg, gf, final, tm):
    t, d = h.shape
    row = lambda width: pl.BlockSpec((tm, width), lambda i: (i, 0))
    return pl.pallas_call(
        functools.partial(_post_kernel, ff_chunk=1024, final=final),
        grid=(t // tm,),
        in_specs=[row(d), row(y1.shape[1]), row(y2.shape[1]), _resident(wo.shape), _resident((1, d)),
                  _resident(w1.shape), _resident(w2.shape), _resident((1, d)), row(p.shape[1]),
                  _resident(wp.shape), _resident(wg.shape), _resident((1, d))],
        out_specs=row(d),
        out_shape=jax.ShapeDtypeStruct((t, d), F32),
        compiler_params=_cparams(("parallel",)),
        name="post_mixer",
    )(h, y1, y2, wo, gm.reshape(1, d), w1, w2, gp.reshape(1, d), p, wp, wg, gf.reshape(1, d))


def _stack_heads(x, scale):
    lane = lax.broadcasted_iota(jnp.int32, x.shape, 1)
    zero = jnp.zeros_like(x)
    lo = jnp.where(lane < 64, x, zero)
    hi = jnp.where(lane >= 64, x, zero)
    return jnp.concatenate([lo, hi], axis=0) * scale


def _diff_attn_kernel(q_ref, k_ref, v_ref, lq1_ref, lk1_ref, lq2_ref, lk2_ref, sg_ref, bias_ref, o_ref,
                      m_s, l_s, acc_s, *,
                      tq, bk, lam_init):
    qi = pl.program_id(2)
    heads = m_s.shape[0]
    lanes = lambda h: pl.ds(h * LANES, LANES)
    scale = jnp.asarray(DIFF_DH ** -0.5, BF16)
    qs = [_stack_heads(q_ref[:, lanes(h)], scale) for h in range(heads)]
    n_full = (qi * tq) // bk

    def scores(j):
        keys = pl.ds(pl.multiple_of(j * bk, bk), bk)
        return tuple(lax.dot_general(qs[h], k_ref[keys, lanes(h)], (((1,), (1,)), ((), ())),
                                     preferred_element_type=F32) for h in range(heads))

    def update(j, ss, masked):
        keys = pl.ds(pl.multiple_of(j * bk, bk), bk)
        for h in range(heads):
            s = ss[h]
            if masked:
                s = s + bias_ref[(qi * tq - j * bk) // tq]
            m = m_s[h]
            m_new = jnp.maximum(m, jnp.max(s, axis=-1, keepdims=True))
            alpha = jnp.exp(m - m_new)
            p = jnp.exp(s - jnp.concatenate([m_new] * (bk // LANES), axis=1))
            m_s[h] = m_new
            l_s[h] = alpha * l_s[h] + sum(p[:, c * LANES:(c + 1) * LANES] for c in range(bk // LANES))
            acc_s[h] = alpha * acc_s[h] + jnp.dot(p.astype(BF16), v_ref[keys, lanes(h)],
                                                  preferred_element_type=F32)

    m_s[...] = jnp.full(m_s.shape, NEG_BIG, F32)
    l_s[...] = jnp.zeros(l_s.shape, F32)
    acc_s[...] = jnp.zeros(acc_s.shape, F32)

    def body(j, ss):
        nxt = scores(j + 1)
        update(j, ss, False)
        return nxt

    ss = lax.fori_loop(0, n_full, body, scores(0))
    update(n_full, ss, True)
    lam = (jnp.exp(jnp.sum(lq1_ref[...] * lk1_ref[...], keepdims=True))
           - jnp.exp(jnp.sum(lq2_ref[...] * lk2_ref[...], keepdims=True)) + lam_init)
    for h in range(heads):
        o = acc_s[h] / jnp.sum(l_s[h], axis=-1, keepdims=True)
        o = o[:tq] - lam * o[tq:]
        o = o * lax.rsqrt(jnp.mean(o * o, axis=-1, keepdims=True) + 1e-5) * sg_ref[...]
        o_ref[:, lanes(h)] = (o * (1.0 - lam_init)).astype(o_ref.dtype)


def _diff_attn(proj, lq1, lk1, lq2, lk2, sub_gain, lam_init, tq, bk, heads):
    b, s, _ = proj.shape
    groups, width = DIFF_HEADS // heads, heads * LANES
    vec = lambda a: a.astype(F32).reshape(1, -1)
    off = jnp.arange(bk // tq)[:, None, None] * tq
    qpos = off + (jnp.arange(2 * tq) % tq)[None, :, None]
    bias = jnp.where(jnp.arange(bk)[None, None, :] <= qpos, 0.0, NEG_BIG).astype(F32)
    return pl.pallas_call(
        functools.partial(_diff_attn_kernel, tq=tq, bk=bk, lam_init=lam_init),
        grid=(b, groups, s // tq),
        in_specs=[pl.BlockSpec((None, tq, width), lambda bi, h, qi: (bi, qi, h)),
                  pl.BlockSpec((None, s, width), lambda bi, h, qi: (bi, 0, groups + h)),
                  pl.BlockSpec((None, s, width), lambda bi, h, qi: (bi, 0, 2 * groups + h)),
                  _resident((1, DIFF_DH)), _resident((1, DIFF_DH)), _resident((1, DIFF_DH)),
                  _resident((1, DIFF_DH)), _resident((1, 2 * DIFF_DH)), _resident(bias.shape)],
        out_specs=pl.BlockSpec((None, tq, width), lambda bi, h, qi: (bi, qi, h)),
        out_shape=jax.ShapeDtypeStruct((b, s, DIFF_WIDTH), BF16),
        scratch_shapes=[pltpu.VMEM((heads, 2 * tq, LANES), F32)] * 3,
        compiler_params=_cparams(("parallel", "parallel", "arbitrary")),
        name="diff_attn",
    )(proj, proj, proj, vec(lq1), vec(lk1), vec(lq2), vec(lk2), vec(sub_gain), bias)


def _dil_attn_kernel(q_ref, kp_ref, kc_ref, vp_ref, vc_ref, o_ref, k_s, v_s, acc_s, m_s, l_s, bias_s):
    first = pl.program_id(2) == 0
    tile = q_ref.shape[0]
    k_s[pl.ds(0, tile), :] = kp_ref[...]
    k_s[pl.ds(tile, tile), :] = kc_ref[...]
    v_s[pl.ds(0, tile), :] = vp_ref[...]
    v_s[pl.ds(tile, tile), :] = vc_ref[...]
    scale = jnp.asarray(SWA_DH ** -0.5, BF16)
    lane = lax.broadcasted_iota(jnp.int32, (N_BACK, LANES), 1)
    lo = lane < 64
    r = lax.broadcasted_iota(jnp.int32, (2 * N_BACK, 2 * N_BACK), 0)
    qidx = jnp.where(r >= N_BACK, r - N_BACK, r)
    kk = lax.broadcasted_iota(jnp.int32, (2 * N_BACK, 2 * N_BACK), 1)
    band = (kk >= qidx) & (kk <= qidx + N_BACK)
    bias_s[0] = jnp.where(band, 0.0, NEG_BIG)
    bias_s[1] = jnp.where(band & jnp.logical_or(jnp.logical_not(first), kk >= N_BACK), 0.0, NEG_BIG)

    n_sub = DIL_TILE // N_BACK
    for dil in DILATIONS:

        def windows(n, dil=dil):
            st = (n % dil) + (n // dil) * (N_BACK * dil)
            if dil == 1 and not isinstance(n, int):
                st = pl.multiple_of(st, N_BACK)
            kbase = tile + st - N_BACK * dil
            if dil > 1:
                return pl.ds(st, N_BACK, stride=dil), pl.ds(kbase, 2 * N_BACK, stride=dil), kbase
            return pl.ds(st, N_BACK), pl.ds(kbase, 2 * N_BACK), kbase

        def scores(n):
            rows, krows, _ = windows(n)
            qs = _stack_heads(q_ref[rows, :].astype(BF16), scale)
            kb = k_s[krows, :].astype(BF16)
            return lax.dot_general(qs, kb, (((1,), (1,)), ((), ())), preferred_element_type=F32)

        def finish(n, s, residue_head, dil=dil):
            rows, krows, _ = windows(n)
            vb = v_s[krows, :].astype(BF16)
            s = s + bias_s[1 if residue_head else 0]
            m2 = jnp.max(s, axis=-1, keepdims=True)
            p = jnp.exp(s - m2)
            l2 = jnp.sum(p, axis=-1, keepdims=True)
            pv = jnp.dot(p.astype(BF16), vb, preferred_element_type=F32)
            acc_c = jnp.where(lo, pv[:N_BACK], pv[N_BACK:])
            m_c = jnp.where(lo, m2[:N_BACK], m2[N_BACK:])
            l_c = jnp.where(lo, l2[:N_BACK], l2[N_BACK:])
            if dil == DILATIONS[0]:
                acc_s[rows, :] = acc_c
                m_s[rows, :] = m_c
                l_s[rows, :] = l_c
            else:
                m_o = m_s[rows, :]
                m_n = jnp.maximum(m_o, m_c)
                a_o = jnp.exp(m_o - m_n)
                a_c = jnp.exp(m_c - m_n)
                acc_n = a_o * acc_s[rows, :] + a_c * acc_c
                l_n = a_o * l_s[rows, :] + a_c * l_c
                if dil == DILATIONS[-1]:
                    o_ref[rows, :] = (acc_n / l_n).astype(o_ref.dtype)
                else:
                    acc_s[rows, :] = acc_n
                    m_s[rows, :] = m_n
                    l_s[rows, :] = l_n

        n_grp = n_sub // DIL_GROUP
        head_flags = [tuple(g * DIL_GROUP + e < dil for e in range(DIL_GROUP)) for g in range(n_grp)]

        def score_group(g, scores=scores):
            return tuple(scores(g * DIL_GROUP + e) for e in range(DIL_GROUP))

        def finish_group(g, ss, flags, finish=finish):
            for e in range(DIL_GROUP):
                finish(g * DIL_GROUP + e, ss[e], flags[e])

        def body(g, ss, flags, score_group=score_group, finish_group=finish_group):
            nxt = score_group(g + 1)
            finish_group(g, ss, flags)
            return nxt

        ss = score_group(0)
        g0 = 0
        while g0 < n_grp - 1:
            g1 = g0 + 1
            while g1 < n_grp - 1 and head_flags[g1] == head_flags[g0]:
                g1 += 1
            ss = lax.fori_loop(g0, g1, functools.partial(body, flags=head_flags[g0]), ss)
            g0 = g1
        finish_group(n_grp - 1, ss, head_flags[-1])


def _dil_attn(proj):
    b, s, _ = proj.shape
    pairs = SWA_WIDTH // LANES
    cur = lambda part: pl.BlockSpec((None, DIL_TILE, LANES), lambda bi, hp, i: (bi, i, part * pairs + hp))
    prev = lambda part: pl.BlockSpec((None, DIL_TILE, LANES),
                                     lambda bi, hp, i: (bi, jnp.maximum(i - 1, 0), part * pairs + hp))
    return pl.pallas_call(
        _dil_attn_kernel,
        grid=(b, pairs, s // DIL_TILE),
        in_specs=[cur(0), prev(1), cur(1), prev(2), cur(2)],
        out_specs=pl.BlockSpec((None, DIL_TILE, LANES), lambda bi, hp, i: (bi, i, hp)),
        out_shape=jax.ShapeDtypeStruct((b, s, SWA_WIDTH), BF16),
        scratch_shapes=[pltpu.VMEM((2 * DIL_TILE, LANES), F32), pltpu.VMEM((2 * DIL_TILE, LANES), F32),
                        pltpu.VMEM((DIL_TILE, LANES), F32), pltpu.VMEM((DIL_TILE, LANES), F32),
                        pltpu.VMEM((DIL_TILE, LANES), F32), pltpu.VMEM((2, 2 * N_BACK, 2 * N_BACK), F32)],
        compiler_params=_cparams(("parallel", "parallel", "arbitrary")),
        name="dilated_attn",
    )(proj, proj, proj, proj, proj)


def _bundle_block_diag(x):
    nb, gb = SSM_BUNDLES, SSM_GROUPS // SSM_BUNDLES
    _, c, p = x.shape
    x = x.reshape(nb, gb, c, p)
    out = jnp.zeros((nb, gb, c, gb, p), x.dtype)
    for g in range(gb):
        out = out.at[:, g, :, g, :].set(x[:, g])
    return out.reshape(nb, gb * c, gb * p)


def _s5_prep_kernel(bb_re_ref, bb_im_ref, c_re_ref, c_im_ref, pw_re_ref, pw_im_ref, toep_ref, ts_ref, fs_ref):
    lc = SSM_CHUNK
    hp = lax.Precision.HIGHEST
    nt = (((1,), (1,)), ((), ()))
    bb_re, bb_im, c_re, c_im = bb_re_ref[...], bb_im_ref[...], c_re_ref[...], c_im_ref[...]
    half = bb_re.shape[1]
    blk = lambda i: pl.ds(i * LANES, LANES)
    zero = jnp.zeros((LANES, LANES), toep_ref.dtype)
    for j in range(lc):
        a_re, a_im = pw_re_ref[j:j + 1, :], pw_im_ref[j:j + 1, :]
        ab_re = bb_re * a_re - bb_im * a_im
        ab_im = bb_re * a_im + bb_im * a_re
        lag = (lax.dot_general(ab_re, c_re, nt, precision=hp, preferred_element_type=F32)
               - lax.dot_general(ab_im, c_im, nt, precision=hp, preferred_element_type=F32))
        lag = lag.astype(toep_ref.dtype)
        for s in range(lc - j):
            toep_ref[blk(s), blk(s + j)] = lag
        for t in range(j):
            toep_ref[blk(j), blk(t)] = zero
        s = lc - 1 - j
        ts_ref[blk(s), pl.ds(0, half)] = ab_re.astype(ts_ref.dtype)
        ts_ref[blk(s), pl.ds(half, half)] = ab_im.astype(ts_ref.dtype)
    for t in range(lc):
        a_re, a_im = pw_re_ref[t + 1:t + 2, :], pw_im_ref[t + 1:t + 2, :]
        ca_re = c_re * a_re - c_im * a_im
        ca_im = c_re * a_im + c_im * a_re
        fs_ref[pl.ds(0, half), blk(t)] = ca_re.T.astype(fs_ref.dtype)
        fs_ref[pl.ds(half, half), blk(t)] = (-ca_im).T.astype(fs_ref.dtype)


def _s5_operators(lam_re, lam_im, log_dt, b_re, b_im, c_re, c_im, d_skip):
    lc = SSM_CHUNK
    nb, gb, st = SSM_BUNDLES, SSM_GROUPS // SSM_BUNDLES, SSM_STATE
    lr, li = lam_re.astype(F32), lam_im.astype(F32)
    dt = jnp.exp(log_dt.astype(F32))[:, None]
    mag = jnp.exp(lr * dt)
    abar_re, abar_im = mag * jnp.cos(li * dt), mag * jnp.sin(li * dt)
    den = lr * lr + li * li
    nr, ni = abar_re - 1.0, abar_im
    coef_re = (nr * lr + ni * li) / den
    coef_im = (ni * lr - nr * li) / den
    br, bi = b_re.astype(F32), b_im.astype(F32)
    bb_re = coef_re[..., None] * br - coef_im[..., None] * bi
    bb_im = coef_re[..., None] * bi + coef_im[..., None] * br
    j = jnp.arange(lc + 1, dtype=F32)[:, None, None]
    pmag = jnp.exp(j * (lr * dt)[None])
    pw_re, pw_im = pmag * jnp.cos(j * (li * dt)[None]), pmag * jnp.sin(j * (li * dt)[None])
    per_bundle = lambda pw: jnp.transpose(pw.reshape(lc + 1, nb, gb * st), (1, 0, 2))
    pw_re, pw_im = per_bundle(pw_re), per_bundle(pw_im)
    small = [_bundle_block_diag(jnp.swapaxes(bb_re, 1, 2)), _bundle_block_diag(jnp.swapaxes(bb_im, 1, 2)),
             _bundle_block_diag(c_re.astype(F32)), _bundle_block_diag(c_im.astype(F32)), pw_re, pw_im]
    width, states = lc * LANES, 2 * gb * st
    bundle = lambda shape: pl.BlockSpec((None,) + shape, lambda bi: (bi, 0, 0))
    toep, to_state, from_state = pl.pallas_call(
        _s5_prep_kernel,
        grid=(nb,),
        in_specs=[bundle(a.shape[1:]) for a in small],
        out_specs=[bundle((width, width)), bundle((width, states)), bundle((states, width))],
        out_shape=[jax.ShapeDtypeStruct((nb, width, width), BF16), jax.ShapeDtypeStruct((nb, width, states), BF16),
                   jax.ShapeDtypeStruct((nb, states, width), BF16)],
        compiler_params=_cparams(("parallel",)),
        name="s5_prep",
    )(*small)
    a_lc_re, a_lc_im = pw_re[:, lc], pw_im[:, lc]
    dec_a = jnp.concatenate([a_lc_re, a_lc_re], axis=-1)[:, None, :]
    dec_b = jnp.concatenate([-a_lc_im, a_lc_im], axis=-1)[:, None, :]
    skip = d_skip.astype(F32).reshape(nb, 1, LANES)
    return toep, to_state, from_state, dec_a, dec_b, skip


def _s5_kernel(u_ref, toep_ref, ts_ref, fs_ref, da_ref, db_ref, skip_ref, y_ref, lhs_s, ug_s, loc_s, cin_s,
               state_s):
    lc = SSM_CHUNK
    batch, tile, _ = u_ref.shape
    chunks = tile // lc
    slabs = state_s.shape[0]

    for b in range(batch):
        for s in range(lc):
            piece = u_ref[b, pl.ds(s, chunks, stride=lc), :]
            ug_s[s, pl.ds(b * chunks, chunks), :] = piece
            lhs_s[pl.ds(b * chunks, chunks), pl.ds(s * LANES, LANES)] = piece.astype(BF16)
    lhs = lhs_s[...]
    y = jnp.dot(lhs, toep_ref[...], preferred_element_type=F32)
    loc = jnp.dot(lhs, ts_ref[...], preferred_element_type=F32)
    for k in range(slabs):
        loc_s[k] = loc[:, k * LANES:(k + 1) * LANES]

    @pl.when(pl.program_id(1) == 0)
    def _():
        state_s[...] = jnp.zeros(state_s.shape, F32)

    da = [da_ref[:, k * LANES:(k + 1) * LANES] for k in range(slabs)]
    db = [db_ref[:, k * LANES:(k + 1) * LANES] for k in range(slabs)]

    def step(n, x):
        rows = pl.ds(n, batch, stride=chunks)
        for k in range(slabs):
            cin_s[k, rows, :] = x[k]
        return tuple(da[k] * x[k] + db[k] * x[(k + slabs // 2) % slabs] + loc_s[k, rows, :] for k in range(slabs))

    x = lax.fori_loop(0, chunks, step, tuple(state_s[k] for k in range(slabs)))
    for k in range(slabs):
        state_s[k] = x[k]
    cin = jnp.concatenate([cin_s[k] for k in range(slabs)], axis=1).astype(BF16)
    y = y + jnp.dot(cin, fs_ref[...], preferred_element_type=F32)
    skip = skip_ref[...]
    for b in range(batch):
        for t in range(lc):
            rows = pl.ds(b * chunks, chunks)
            y_ref[b, pl.ds(t, chunks, stride=lc), :] = (y[b * chunks:(b + 1) * chunks, t * LANES:(t + 1) * LANES]
                                                        + ug_s[t, rows, :] * skip)


def _s5_scan(u, ops, tile):
    toep, to_state, from_state, dec_a, dec_b, skip = ops
    batch, seq, _ = u.shape
    lc = SSM_CHUNK
    rows = batch * (tile // lc)
    slabs = to_state.shape[2] // LANES
    act = pl.BlockSpec((batch, tile, LANES), lambda bi, si: (0, si, bi))
    per_bundle = lambda a, **kw: pl.BlockSpec((None,) + a.shape[1:], lambda bi, si: (bi, 0, 0), **kw)
    once = dict(pipeline_mode=pl.Buffered(1))
    return pl.pallas_call(
        _s5_kernel,
        grid=(SSM_BUNDLES, seq // tile),
        in_specs=[act, per_bundle(toep, **once), per_bundle(to_state, **once), per_bundle(from_state, **once),
                  per_bundle(dec_a), per_bundle(dec_b), per_bundle(skip)],
        out_specs=act,
        out_shape=jax.ShapeDtypeStruct(u.shape, F32),
        scratch_shapes=[pltpu.VMEM((rows, lc * LANES), BF16), pltpu.VMEM((lc, rows, LANES), F32),
                        pltpu.VMEM((slabs, rows, LANES), F32), pltpu.VMEM((slabs, rows, LANES), F32),
                        pltpu.VMEM((slabs, batch, LANES), F32)],
        compiler_params=_cparams(("parallel", "arbitrary")),
        name="s5_scan",
    )(u, toep, to_state, from_state, dec_a, dec_b, skip)


def _odd_mix_kernel(ys_ref, gb_ref, gc_ref, xt_ref, gch_ref, xth_ref, wglu_ref, cw_ref, yc_ref, yd_ref, *, tiles_per_seq):
    y = _gelu_tanh(ys_ref[...])
    yc_ref[...] = (y * _sigmoid(jnp.dot(y.astype(BF16), wglu_ref[...], preferred_element_type=F32))).astype(yc_ref.dtype)
    z = gc_ref[...] * xt_ref[...]
    seq_start = (pl.program_id(0) % tiles_per_seq) == 0
    zh = jnp.where(seq_start, 0.0, gch_ref[...] * xth_ref[...])
    row = lax.broadcasted_iota(jnp.int32, z.shape, 0)
    z1 = jnp.where(row == 0, zh[7:8, :], pltpu.roll(z, 1, 0))
    z2 = jnp.where(row == 0, zh[6:7, :], jnp.where(row == 1, zh[7:8, :], pltpu.roll(z, 2, 0)))
    cw = cw_ref[...]
    conv = cw[0:1, :] * z + cw[1:2, :] * z1 + cw[2:3, :] * z2
    yd_ref[...] = (gb_ref[...] * conv).astype(yd_ref.dtype)


def _odd_mix(y_ssm, proj, w_glu, conv_w, seq, tm):
    t = y_ssm.shape[0]
    w = CONV_WIDTH
    halo = 8
    col = lambda c: pl.BlockSpec((tm, w), lambda i: (i, c))
    prev = lambda c: pl.BlockSpec((halo, w), lambda i: (jnp.maximum(i * (tm // halo) - 1, 0), c))
    return pl.pallas_call(
        functools.partial(_odd_mix_kernel, tiles_per_seq=seq // tm),
        grid=(t // tm,),
        in_specs=[col(0), col(0), col(1), col(2), prev(1), prev(2), _resident(w_glu.shape), _resident(conv_w.shape)],
        out_specs=[col(0), col(0)],
        out_shape=[jax.ShapeDtypeStruct((t, SSM_WIDTH), BF16), jax.ShapeDtypeStruct((t, w), BF16)],
        compiler_params=_cparams(("parallel",)),
        name="odd_mix",
    )(y_ssm, proj, proj, proj, proj, proj, w_glu, conv_w)


def kernel(x, p, norm_mix, norm_mlp, norm_ple, w_mlp_in, w_mlp_out, w_ple_proj, w_ple_gate, attn_w_in, attn_w_out, diff_lq1, diff_lk1, diff_lq2, diff_lk2, diff_sub_gain, rc_w_in, rc_w_out, ssm_lambda_re, ssm_lambda_im, ssm_log_dt, ssm_b_re, ssm_b_im, ssm_c_re, ssm_c_im, ssm_d, ssm_w_glu, conv_w, norm_final):
    b, s, d = x.shape
    depth = p.shape[0]
    t = b * s
    tm = 512
    assert s % DIL_TILE == 0 and s % tm == 0 and d == 2 * DIFF_WIDTH
    h = x.reshape(t, d)
    for i in range(depth):
        if i % 2 == 0:
            e = i // 2
            lam_init = 0.8 - 0.6 * math.exp(-0.3 * i)
            pa, pb = _norm_matmul(h, norm_mix[i], attn_w_in[e].astype(BF16),
                                  ((0, 3 * DIFF_WIDTH), (3 * DIFF_WIDTH, 3 * DIFF_WIDTH + 3 * SWA_WIDTH)),
                                  (BF16, F32), tm)
            y1 = _diff_attn(pa.reshape(b, s, -1), diff_lq1[e], diff_lk1[e], diff_lq2[e], diff_lk2[e],
                            diff_sub_gain[e], lam_init, tq=256, bk=512, heads=2).reshape(t, -1)
            y2 = _dil_attn(pb.reshape(b, s, -1)).reshape(t, -1)
            wo = attn_w_out[e]
        else:
            o = i // 2
            u, pr = _norm_matmul(h, norm_mix[i], rc_w_in[o].astype(BF16),
                                 ((0, SSM_WIDTH), (SSM_WIDTH, SSM_WIDTH + 3 * CONV_WIDTH)), (F32, F32), tm)
            ops = _s5_operators(ssm_lambda_re[o], ssm_lambda_im[o], ssm_log_dt[o], ssm_b_re[o], ssm_b_im[o],
                                ssm_c_re[o], ssm_c_im[o], ssm_d[o])
            y_ssm = _s5_scan(u.reshape(b, s, -1), ops, min(SSM_TILE, s)).reshape(t, -1)
            y1, y2 = _odd_mix(y_ssm, pr, ssm_w_glu[o].astype(BF16), conv_w[o].astype(F32), s, tm)
            wo = rc_w_out[o]
        h = _post(h, y1, y2, wo.astype(BF16), norm_mlp[i], w_mlp_in[i].astype(BF16), w_mlp_out[i].astype(BF16),
                  norm_ple[i], p[i].reshape(t, -1), w_ple_proj[i].astype(BF16), w_ple_gate[i].astype(BF16),
                  norm_final, i == depth - 1, tm)
    return h.reshape(b, s, d)
```

```python
import functools
import math

import jax
import jax.numpy as jnp
from jax import lax
from jax.experimental import pallas as pl
from jax.experimental.pallas import tpu as pltpu

F32 = jnp.float32
BF16 = jnp.bfloat16

EPS = 1e-6
PLE_DIM = 256
DIFF_DH = 64
DIFF_HEADS = 4
DIFF_WIDTH = 512
SWA_DH = 64
SWA_WIDTH = 512
DILATIONS = (16, 4, 1)
N_BACK = 128
DIL_TILE = N_BACK * 16
DIL_GROUP = 4
SSM_GROUP_CH = 16
SSM_STATE = 64
SSM_WIDTH = 512
SSM_GROUPS = 32
SSM_BUNDLES = 4
SSM_CHUNK = 16
SSM_TILE = 1024
CONV_WIDTH = 512
LANES = 128
NEG_BIG = -1e30
VMEM_LIMIT = 60 * 1024 * 1024


def _cparams(sem):
    return pltpu.CompilerParams(dimension_semantics=sem, vmem_limit_bytes=VMEM_LIMIT)


def _resident(shape):
    nd = len(shape)
    return pl.BlockSpec(shape, lambda *_: (0,) * nd, pipeline_mode=pl.Buffered(1))


def _rms(x, g, eps=EPS):
    return x * lax.rsqrt(jnp.mean(x * x, axis=-1, keepdims=True) + eps) * g


def _sigmoid(x):
    return 1.0 / (1.0 + jnp.exp(-x))


def _gelu_tanh(x):
    c = math.sqrt(2.0 / math.pi)
    return x * (0.5 * (1.0 + jnp.tanh(c * (x + 0.044715 * (x * x * x)))))


def _norm_matmul_kernel(h_ref, g_ref, w_ref, *o_refs, splits):
    xn = _rms(h_ref[...], g_ref[...]).astype(BF16)
    for o_ref, (c0, c1) in zip(o_refs, splits):
        o_ref[...] = jnp.dot(xn, w_ref[:, c0:c1], preferred_element_type=F32).astype(o_ref.dtype)


def _norm_matmul(h, g, w, splits, dtypes, tm):
    t, d = h.shape
    n = w.shape[1]
    return pl.pallas_call(
        functools.partial(_norm_matmul_kernel, splits=splits),
        grid=(t // tm,),
        in_specs=[pl.BlockSpec((tm, d), lambda i: (i, 0)), _resident((1, d)), _resident((d, n))],
        out_specs=[pl.BlockSpec((tm, c1 - c0), lambda i: (i, 0)) for (c0, c1) in splits],
        out_shape=[jax.ShapeDtypeStruct((t, c1 - c0), dt) for (c0, c1), dt in zip(splits, dtypes)],
        compiler_params=_cparams(("parallel",)),
        name="norm_matmul",
    )(h, g.reshape(1, d), w)


def _post_kernel(h_ref, y1_ref, y2_ref, wo_ref, gm_ref, w1_ref, w2_ref, gp_ref, p_ref, wp_ref, wg_ref,
                 gf_ref, o_ref, *, ff_chunk, final):
    k1 = y1_ref.shape[1]
    h = h_ref[...]
    h = h + jnp.dot(y1_ref[...], wo_ref[:k1, :], preferred_element_type=F32)
    h = h + jnp.dot(y2_ref[...], wo_ref[k1:, :], preferred_element_type=F32)
    hn = _rms(h, gm_ref[...]).astype(BF16)
    d_ff = w1_ref.shape[1]
    acc = jnp.zeros_like(h)
    for c in range(d_ff // ff_chunk):
        a = jnp.dot(hn, w1_ref[:, c * ff_chunk:(c + 1) * ff_chunk], preferred_element_type=F32)
        a = jnp.square(jnp.maximum(a, 0.0)).astype(BF16)
        acc = acc + jnp.dot(a, w2_ref[c * ff_chunk:(c + 1) * ff_chunk, :], preferred_element_type=F32)
    h = h + acc
    hn = _rms(h, gp_ref[...]).astype(BF16)
    gate = _sigmoid(jnp.dot(hn, wg_ref[...], preferred_element_type=F32))
    emb = jnp.dot(p_ref[...].astype(BF16), wp_ref[...], preferred_element_type=F32)
    h = h + emb * gate
    if final:
        h = _rms(h, gf_ref[...])
    o_ref[...] = h


def _post(h, y1, y2, wo, gm, w1, w2, gp, p, layer, wp, wg, gf, final, tm):
    t, d = h.shape
    row = lambda width: pl.BlockSpec((tm, width), lambda i: (i, 0))
    p_row = pl.BlockSpec((tm, p.shape[1]), lambda i: (layer * (t // tm) + i, 0))
    return pl.pallas_call(
        functools.partial(_post_kernel, ff_chunk=1024, final=final),
        grid=(t // tm,),
        in_specs=[row(d), row(y1.shape[1]), row(y2.shape[1]), _resident(wo.shape), _resident((1, d)),
                  _resident(w1.shape), _resident(w2.shape), _resident((1, d)), p_row,
                  _resident(wp.shape), _resident(wg.shape), _resident((1, d))],
        out_specs=row(d),
        out_shape=jax.ShapeDtypeStruct((t, d), F32),
        compiler_params=_cparams(("parallel",)),
        name="post_mixer",
    )(h, y1, y2, wo, gm.reshape(1, d), w1, w2, gp.reshape(1, d), p, wp, wg, gf.reshape(1, d))


def _stack_heads(x, scale):
    lane = lax.broadcasted_iota(jnp.int32, x.shape, 1)
    zero = jnp.zeros_like(x)
    lo = jnp.where(lane < 64, x, zero)
    hi = jnp.where(lane >= 64, x, zero)
    return jnp.concatenate([lo, hi], axis=0) * scale


def _diff_attn_kernel(q_ref, k_ref, v_ref, lq1_ref, lk1_ref, lq2_ref, lk2_ref, sg_ref, bias_ref, o_ref,
                      m_s, l_s, acc_s, s_s, *, tq, bk, lam_init):
    qi = pl.program_id(2)
    heads = m_s.shape[0]
    lanes = lambda h: pl.ds(h * LANES, LANES)
    scale = jnp.asarray(DIFF_DH ** -0.5, BF16)
    qs = [_stack_heads(q_ref[:, lanes(h)], scale) for h in range(heads)]
    n_full = (qi * tq) // bk

    def scores(j, slot):
        keys = pl.ds(pl.multiple_of(j * bk, bk), bk)
        for h in range(heads):
            s_s[slot, h] = lax.dot_general(qs[h], k_ref[keys, lanes(h)], (((1,), (1,)), ((), ())),
                                           preferred_element_type=F32)

    def update(j, slot, masked):
        keys = pl.ds(pl.multiple_of(j * bk, bk), bk)
        for h in range(heads):
            s = s_s[slot, h]
            if masked:
                s = s + bias_ref[(qi * tq - j * bk) // tq]
            m = m_s[h]
            m_new = jnp.maximum(m, jnp.max(s, axis=-1, keepdims=True))
            alpha = jnp.exp(m - m_new)
            p = jnp.exp(s - jnp.concatenate([m_new] * (bk // LANES), axis=1))
            m_s[h] = m_new
            l_s[h] = alpha * l_s[h] + sum(p[:, c * LANES:(c + 1) * LANES] for c in range(bk // LANES))
            acc_s[h] = alpha * acc_s[h] + jnp.dot(p.astype(BF16), v_ref[keys, lanes(h)],
                                                  preferred_element_type=F32)

    m_s[...] = jnp.full(m_s.shape, NEG_BIG, F32)
    l_s[...] = jnp.zeros(l_s.shape, F32)
    acc_s[...] = jnp.zeros(acc_s.shape, F32)

    scores(0, 0)

    def pair(i, carry):
        j = 2 * i
        scores(j + 1, 1)
        update(j, 0, False)
        scores(j + 2, 0)
        update(j + 1, 1, False)
        return carry

    lax.fori_loop(0, n_full // 2, pair, 0)
    odd = n_full % 2 == 1

    @pl.when(odd)
    def _():
        scores(n_full, 1)
        update(n_full - 1, 0, False)
        update(n_full, 1, True)

    @pl.when(jnp.logical_not(odd))
    def _():
        update(n_full, 0, True)
    lam = (jnp.exp(jnp.sum(lq1_ref[...] * lk1_ref[...], keepdims=True))
           - jnp.exp(jnp.sum(lq2_ref[...] * lk2_ref[...], keepdims=True)) + lam_init)
    for h in range(heads):
        o = acc_s[h] / jnp.sum(l_s[h], axis=-1, keepdims=True)
        o = o[:tq] - lam * o[tq:]
        o = o * lax.rsqrt(jnp.mean(o * o, axis=-1, keepdims=True) + 1e-5) * sg_ref[...]
        o_ref[:, lanes(h)] = (o * (1.0 - lam_init)).astype(o_ref.dtype)


def _diff_attn(proj, lq1, lk1, lq2, lk2, sub_gain, lam_init, tq, bk, heads):
    b, s, _ = proj.shape
    groups, width = DIFF_HEADS // heads, heads * LANES
    vec = lambda a: a.astype(F32).reshape(1, -1)
    off = jnp.arange(bk // tq)[:, None, None] * tq
    qpos = off + (jnp.arange(2 * tq) % tq)[None, :, None]
    bias = jnp.where(jnp.arange(bk)[None, None, :] <= qpos, 0.0, NEG_BIG).astype(F32)
    return pl.pallas_call(
        functools.partial(_diff_attn_kernel, tq=tq, bk=bk, lam_init=lam_init),
        grid=(b, groups, s // tq),
        in_specs=[pl.BlockSpec((None, tq, width), lambda bi, h, qi: (bi, qi, h)),
                  pl.BlockSpec((None, s, width), lambda bi, h, qi: (bi, 0, groups + h)),
                  pl.BlockSpec((None, s, width), lambda bi, h, qi: (bi, 0, 2 * groups + h)),
                  _resident((1, DIFF_DH)), _resident((1, DIFF_DH)), _resident((1, DIFF_DH)),
                  _resident((1, DIFF_DH)), _resident((1, 2 * DIFF_DH)), _resident(bias.shape)],
        out_specs=pl.BlockSpec((None, tq, width), lambda bi, h, qi: (bi, qi, h)),
        out_shape=jax.ShapeDtypeStruct((b, s, DIFF_WIDTH), BF16),
        scratch_shapes=[pltpu.VMEM((heads, 2 * tq, LANES), F32)] * 3 + [pltpu.VMEM((2, heads, 2 * tq, bk), F32)],
        compiler_params=_cparams(("parallel", "parallel", "arbitrary")),
        name="diff_attn",
    )(proj, proj, proj, vec(lq1), vec(lk1), vec(lq2), vec(lk2), vec(sub_gain), bias)


def _dil_attn_kernel(q_ref, kp_ref, kc_ref, vp_ref, vc_ref, o_ref, k_s, v_s, acc_s, m_s, l_s, bias_s, s_s):
    first = pl.program_id(2) == 0
    tile = q_ref.shape[0]
    k_s[pl.ds(0, tile), :] = kp_ref[...]
    k_s[pl.ds(tile, tile), :] = kc_ref[...]
    v_s[pl.ds(0, tile), :] = vp_ref[...]
    v_s[pl.ds(tile, tile), :] = vc_ref[...]
    scale = jnp.asarray(SWA_DH ** -0.5, BF16)
    lane = lax.broadcasted_iota(jnp.int32, (N_BACK, LANES), 1)
    lo = lane < 64
    r = lax.broadcasted_iota(jnp.int32, (2 * N_BACK, 2 * N_BACK), 0)
    qidx = jnp.where(r >= N_BACK, r - N_BACK, r)
    kk = lax.broadcasted_iota(jnp.int32, (2 * N_BACK, 2 * N_BACK), 1)
    band = (kk >= qidx) & (kk <= qidx + N_BACK)
    bias_s[0] = jnp.where(band, 0.0, NEG_BIG)
    bias_s[1] = jnp.where(band & jnp.logical_or(jnp.logical_not(first), kk >= N_BACK), 0.0, NEG_BIG)

    def windows(dil, n):
        st = (n % dil) + (n // dil) * (N_BACK * dil)
        kbase = tile + st - N_BACK * dil
        if dil > 1:
            return pl.ds(st, N_BACK, stride=dil), pl.ds(kbase, 2 * N_BACK, stride=dil)
        return pl.ds(st, N_BACK), pl.ds(kbase, 2 * N_BACK)

    def score(dil, n, slot):
        rows, krows = windows(dil, n)
        qs = _stack_heads(q_ref[rows, :].astype(BF16), scale)
        kb = k_s[krows, :].astype(BF16)
        s_s[slot] = lax.dot_general(qs, kb, (((1,), (1,)), ((), ())), preferred_element_type=F32)

    def finish(dil, n, slot):
        rows, krows = windows(dil, n)
        vb = v_s[krows, :].astype(BF16)
        s = s_s[slot] + bias_s[1 if n < dil else 0]
        m2 = jnp.max(s, axis=-1, keepdims=True)
        p = jnp.exp(s - m2)
        l2 = jnp.sum(p, axis=-1, keepdims=True)
        pv = jnp.dot(p.astype(BF16), vb, preferred_element_type=F32)
        acc_c = jnp.where(lo, pv[:N_BACK], pv[N_BACK:])
        m_c = jnp.where(lo, m2[:N_BACK], m2[N_BACK:])
        l_c = jnp.where(lo, l2[:N_BACK], l2[N_BACK:])
        if dil == DILATIONS[0]:
            acc_s[rows, :] = acc_c
            m_s[rows, :] = m_c
            l_s[rows, :] = l_c
        else:
            m_o = m_s[rows, :]
            m_n = jnp.maximum(m_o, m_c)
            a_o = jnp.exp(m_o - m_n)
            a_c = jnp.exp(m_c - m_n)
            acc_n = a_o * acc_s[rows, :] + a_c * acc_c
            l_n = a_o * l_s[rows, :] + a_c * l_c
            if dil == DILATIONS[-1]:
                o_ref[rows, :] = (acc_n / l_n).astype(o_ref.dtype)
            else:
                acc_s[rows, :] = acc_n
                m_s[rows, :] = m_n
                l_s[rows, :] = l_n

    groups = [(dil, g) for dil in DILATIONS for g in range(DIL_TILE // N_BACK // DIL_GROUP)]

    def score_group(i):
        dil, g = groups[i]
        for e in range(DIL_GROUP):
            score(dil, g * DIL_GROUP + e, (i % 2) * DIL_GROUP + e)

    score_group(0)
    for i, (dil, g) in enumerate(groups):
        if i + 1 < len(groups):
            score_group(i + 1)
        for e in range(DIL_GROUP):
            finish(dil, g * DIL_GROUP + e, (i % 2) * DIL_GROUP + e)


def _dil_attn(proj):
    b, s, _ = proj.shape
    pairs = SWA_WIDTH // LANES
    cur = lambda part: pl.BlockSpec((None, DIL_TILE, LANES), lambda bi, hp, i: (bi, i, part * pairs + hp))
    prev = lambda part: pl.BlockSpec((None, DIL_TILE, LANES),
                                     lambda bi, hp, i: (bi, jnp.maximum(i - 1, 0), part * pairs + hp))
    return pl.pallas_call(
        _dil_attn_kernel,
        grid=(b, pairs, s // DIL_TILE),
        in_specs=[cur(0), prev(1), cur(1), prev(2), cur(2)],
        out_specs=pl.BlockSpec((None, DIL_TILE, LANES), lambda bi, hp, i: (bi, i, hp)),
        out_shape=jax.ShapeDtypeStruct((b, s, SWA_WIDTH), BF16),
        scratch_shapes=[pltpu.VMEM((2 * DIL_TILE, LANES), F32), pltpu.VMEM((2 * DIL_TILE, LANES), F32),
                        pltpu.VMEM((DIL_TILE, LANES), F32), pltpu.VMEM((DIL_TILE, LANES), F32),
                        pltpu.VMEM((DIL_TILE, LANES), F32), pltpu.VMEM((2, 2 * N_BACK, 2 * N_BACK), F32),
                        pltpu.VMEM((2 * DIL_GROUP, 2 * N_BACK, 2 * N_BACK), F32)],
        compiler_params=_cparams(("parallel", "parallel", "arbitrary")),
        name="dilated_attn",
    )(proj, proj, proj, proj, proj)


def _bundle_block_diag(x):
    nb, gb = SSM_BUNDLES, SSM_GROUPS // SSM_BUNDLES
    _, c, p = x.shape
    x = x.reshape(nb, gb, c, p)
    out = jnp.zeros((nb, gb, c, gb, p), x.dtype)
    for g in range(gb):
        out = out.at[:, g, :, g, :].set(x[:, g])
    return out.reshape(nb, gb * c, gb * p)


def _s5_prep_kernel(bb_re_ref, bb_im_ref, c_re_ref, c_im_ref, pw_re_ref, pw_im_ref, toep_ref, ts_ref, fs_ref):
    lc = SSM_CHUNK
    hp = lax.Precision.HIGHEST
    nt = (((1,), (1,)), ((), ()))
    bb_re, bb_im, c_re, c_im = bb_re_ref[...], bb_im_ref[...], c_re_ref[...], c_im_ref[...]
    half = bb_re.shape[1]
    blk = lambda i: pl.ds(i * LANES, LANES)
    zero = jnp.zeros((LANES, LANES), toep_ref.dtype)
    for j in range(lc):
        a_re, a_im = pw_re_ref[j:j + 1, :], pw_im_ref[j:j + 1, :]
        ab_re = bb_re * a_re - bb_im * a_im
        ab_im = bb_re * a_im + bb_im * a_re
        lag = (lax.dot_general(ab_re, c_re, nt, precision=hp, preferred_element_type=F32)
               - lax.dot_general(ab_im, c_im, nt, precision=hp, preferred_element_type=F32))
        lag = lag.astype(toep_ref.dtype)
        for s in range(lc - j):
            toep_ref[blk(s), blk(s + j)] = lag
        for t in range(j):
            toep_ref[blk(j), blk(t)] = zero
        s = lc - 1 - j
        ts_ref[blk(s), pl.ds(0, half)] = ab_re.astype(ts_ref.dtype)
        ts_ref[blk(s), pl.ds(half, half)] = ab_im.astype(ts_ref.dtype)
    for t in range(lc):
        a_re, a_im = pw_re_ref[t + 1:t + 2, :], pw_im_ref[t + 1:t + 2, :]
        ca_re = c_re * a_re - c_im * a_im
        ca_im = c_re * a_im + c_im * a_re
        fs_ref[pl.ds(0, half), blk(t)] = ca_re.T.astype(fs_ref.dtype)
        fs_ref[pl.ds(half, half), blk(t)] = (-ca_im).T.astype(fs_ref.dtype)


def _s5_operators(lam_re, lam_im, log_dt, b_re, b_im, c_re, c_im, d_skip):
    lc = SSM_CHUNK
    nb, gb, st = SSM_BUNDLES, SSM_GROUPS // SSM_BUNDLES, SSM_STATE
    lr, li = lam_re.astype(F32), lam_im.astype(F32)
    dt = jnp.exp(log_dt.astype(F32))[:, None]
    mag = jnp.exp(lr * dt)
    abar_re, abar_im = mag * jnp.cos(li * dt), mag * jnp.sin(li * dt)
    den = lr * lr + li * li
    nr, ni = abar_re - 1.0, abar_im
    coef_re = (nr * lr + ni * li) / den
    coef_im = (ni * lr - nr * li) / den
    br, bi = b_re.astype(F32), b_im.astype(F32)
    bb_re = coef_re[..., None] * br - coef_im[..., None] * bi
    bb_im = coef_re[..., None] * bi + coef_im[..., None] * br
    j = jnp.arange(lc + 1, dtype=F32)[:, None, None]
    pmag = jnp.exp(j * (lr * dt)[None])
    pw_re, pw_im = pmag * jnp.cos(j * (li * dt)[None]), pmag * jnp.sin(j * (li * dt)[None])
    per_bundle = lambda pw: jnp.transpose(pw.reshape(lc + 1, nb, gb * st), (1, 0, 2))
    pw_re, pw_im = per_bundle(pw_re), per_bundle(pw_im)
    small = [_bundle_block_diag(jnp.swapaxes(bb_re, 1, 2)), _bundle_block_diag(jnp.swapaxes(bb_im, 1, 2)),
             _bundle_block_diag(c_re.astype(F32)), _bundle_block_diag(c_im.astype(F32)), pw_re, pw_im]
    width, states = lc * LANES, 2 * gb * st
    bundle = lambda shape: pl.BlockSpec((None,) + shape, lambda bi: (bi, 0, 0))
    toep, to_state, from_state = pl.pallas_call(
        _s5_prep_kernel,
        grid=(nb,),
        in_specs=[bundle(a.shape[1:]) for a in small],
        out_specs=[bundle((width, width)), bundle((width, states)), bundle((states, width))],
        out_shape=[jax.ShapeDtypeStruct((nb, width, width), BF16), jax.ShapeDtypeStruct((nb, width, states), BF16),
                   jax.ShapeDtypeStruct((nb, states, width), BF16)],
        compiler_params=_cparams(("parallel",)),
        name="s5_prep",
    )(*small)
    a_lc_re, a_lc_im = pw_re[:, lc], pw_im[:, lc]
    dec_a = jnp.concatenate([a_lc_re, a_lc_re], axis=-1)[:, None, :]
    dec_b = jnp.concatenate([-a_lc_im, a_lc_im], axis=-1)[:, None, :]
    skip = d_skip.astype(F32).reshape(nb, 1, LANES)
    return toep, to_state, from_state, dec_a, dec_b, skip


def _s5_kernel(u_ref, toep_ref, ts_ref, fs_ref, da_ref, db_ref, skip_ref, y_ref, lhs_s, ug_s, loc_s, cin_s,
               state_s):
    lc = SSM_CHUNK
    batch, tile, _ = u_ref.shape
    chunks = tile // lc
    slabs = state_s.shape[0]

    for b in range(batch):
        for s in range(lc):
            piece = u_ref[b, pl.ds(s, chunks, stride=lc), :]
            ug_s[s, pl.ds(b * chunks, chunks), :] = piece
            lhs_s[pl.ds(b * chunks, chunks), pl.ds(s * LANES, LANES)] = piece.astype(BF16)
    lhs = lhs_s[...]
    y = jnp.dot(lhs, toep_ref[...], preferred_element_type=F32)
    loc = jnp.dot(lhs, ts_ref[...], preferred_element_type=F32)
    for k in range(slabs):
        loc_s[k] = loc[:, k * LANES:(k + 1) * LANES]

    @pl.when(pl.program_id(1) == 0)
    def _():
        state_s[...] = jnp.zeros(state_s.shape, F32)

    da = [da_ref[:, k * LANES:(k + 1) * LANES] for k in range(slabs)]
    db = [db_ref[:, k * LANES:(k + 1) * LANES] for k in range(slabs)]

    def step(n, x):
        rows = pl.ds(n, batch, stride=chunks)
        for k in range(slabs):
            cin_s[k, rows, :] = x[k]
        return tuple(da[k] * x[k] + db[k] * x[(k + slabs // 2) % slabs] + loc_s[k, rows, :] for k in range(slabs))

    x = lax.fori_loop(0, chunks, step, tuple(state_s[k] for k in range(slabs)))
    for k in range(slabs):
        state_s[k] = x[k]
    cin = jnp.concatenate([cin_s[k] for k in range(slabs)], axis=1).astype(BF16)
    y = y + jnp.dot(cin, fs_ref[...], preferred_element_type=F32)
    skip = skip_ref[...]
    for b in range(batch):
        for t in range(lc):
            rows = pl.ds(b * chunks, chunks)
            y_ref[b, pl.ds(t, chunks, stride=lc), :] = (y[b * chunks:(b + 1) * chunks, t * LANES:(t + 1) * LANES]
                                                        + ug_s[t, rows, :] * skip)


def _s5_scan(u, ops, tile):
    toep, to_state, from_state, dec_a, dec_b, skip = ops
    batch, seq, _ = u.shape
    lc = SSM_CHUNK
    rows = batch * (tile // lc)
    slabs = to_state.shape[2] // LANES
    act = pl.BlockSpec((batch, tile, LANES), lambda bi, si: (0, si, bi))
    per_bundle = lambda a, **kw: pl.BlockSpec((None,) + a.shape[1:], lambda bi, si: (bi, 0, 0), **kw)
    once = dict(pipeline_mode=pl.Buffered(1))
    return pl.pallas_call(
        _s5_kernel,
        grid=(SSM_BUNDLES, seq // tile),
        in_specs=[act, per_bundle(toep, **once), per_bundle(to_state, **once), per_bundle(from_state, **once),
                  per_bundle(dec_a), per_bundle(dec_b), per_bundle(skip)],
        out_specs=act,
        out_shape=jax.ShapeDtypeStruct(u.shape, F32),
        scratch_shapes=[pltpu.VMEM((rows, lc * LANES), BF16), pltpu.VMEM((lc, rows, LANES), F32),
                        pltpu.VMEM((slabs, rows, LANES), F32), pltpu.VMEM((slabs, rows, LANES), F32),
                        pltpu.VMEM((slabs, batch, LANES), F32)],
        compiler_params=_cparams(("parallel", "arbitrary")),
        name="s5_scan",
    )(u, toep, to_state, from_state, dec_a, dec_b, skip)


def _odd_mix_kernel(ys_ref, gb_ref, gc_ref, xt_ref, gch_ref, xth_ref, wglu_ref, cw_ref, yc_ref, yd_ref, *, tiles_per_seq):
    y = _gelu_tanh(ys_ref[...])
    yc_ref[...] = (y * _sigmoid(jnp.dot(y.astype(BF16), wglu_ref[...], preferred_element_type=F32))).astype(yc_ref.dtype)
    z = gc_ref[...] * xt_ref[...]
    seq_start = (pl.program_id(0) % tiles_per_seq) == 0
    zh = jnp.where(seq_start, 0.0, gch_ref[...] * xth_ref[...])
    row = lax.broadcasted_iota(jnp.int32, z.shape, 0)
    z1 = jnp.where(row == 0, zh[7:8, :], pltpu.roll(z, 1, 0))
    z2 = jnp.where(row == 0, zh[6:7, :], jnp.where(row == 1, zh[7:8, :], pltpu.roll(z, 2, 0)))
    cw = cw_ref[...]
    conv = cw[0:1, :] * z + cw[1:2, :] * z1 + cw[2:3, :] * z2
    yd_ref[...] = (gb_ref[...] * conv).astype(yd_ref.dtype)


def _odd_mix(y_ssm, proj, w_glu, conv_w, seq, tm):
    t = y_ssm.shape[0]
    w = CONV_WIDTH
    halo = 8
    col = lambda c: pl.BlockSpec((tm, w), lambda i: (i, c))
    prev = lambda c: pl.BlockSpec((halo, w), lambda i: (jnp.maximum(i * (tm // halo) - 1, 0), c))
    return pl.pallas_call(
        functools.partial(_odd_mix_kernel, tiles_per_seq=seq // tm),
        grid=(t // tm,),
        in_specs=[col(0), col(0), col(1), col(2), prev(1), prev(2), _resident(w_glu.shape), _resident(conv_w.shape)],
        out_specs=[col(0), col(0)],
        out_shape=[jax.ShapeDtypeStruct((t, SSM_WIDTH), BF16), jax.ShapeDtypeStruct((t, w), BF16)],
        compiler_params=_cparams(("parallel",)),
        name="odd_mix",
    )(y_ssm, proj, proj, proj, proj, proj, w_glu, conv_w)


def kernel(x, p, norm_mix, norm_mlp, norm_ple, w_mlp_in, w_mlp_out, w_ple_proj, w_ple_gate, attn_w_in, attn_w_out, diff_lq1, diff_lk1, diff_lq2, diff_lk2, diff_sub_gain, rc_w_in, rc_w_out, ssm_lambda_re, ssm_lambda_im, ssm_log_dt, ssm_b_re, ssm_b_im, ssm_c_re, ssm_c_im, ssm_d, ssm_w_glu, conv_w, norm_final):
    b, s, d = x.shape
    depth = p.shape[0]
    t = b * s
    tm = 512
    assert s % DIL_TILE == 0 and s % tm == 0 and d == 2 * DIFF_WIDTH
    h = x.reshape(t, d)
    for i in range(depth):
        if i % 2 == 0:
            e = i // 2
            lam_init = 0.8 - 0.6 * math.exp(-0.3 * i)
            pa, pb = _norm_matmul(h, norm_mix[i], attn_w_in[e].astype(BF16),
                                  ((0, 3 * DIFF_WIDTH), (3 * DIFF_WIDTH, 3 * DIFF_WIDTH + 3 * SWA_WIDTH)),
                                  (BF16, F32), tm)
            y1 = _diff_attn(pa.reshape(b, s, -1), diff_lq1[e], diff_lk1[e], diff_lq2[e], diff_lk2[e],
                            diff_sub_gain[e], lam_init, tq=256, bk=512, heads=2).reshape(t, -1)
            y2 = _dil_attn(pb.reshape(b, s, -1)).reshape(t, -1)
            wo = attn_w_out[e]
        else:
            o = i // 2
            u, pr = _norm_matmul(h, norm_mix[i], rc_w_in[o].astype(BF16),
                                 ((0, SSM_WIDTH), (SSM_WIDTH, SSM_WIDTH + 3 * CONV_WIDTH)), (F32, F32), tm)
            ops = _s5_operators(ssm_lambda_re[o], ssm_lambda_im[o], ssm_log_dt[o], ssm_b_re[o], ssm_b_im[o],
                                ssm_c_re[o], ssm_c_im[o], ssm_d[o])
            y_ssm = _s5_scan(u.reshape(b, s, -1), ops, min(SSM_TILE, s)).reshape(t, -1)
            y1, y2 = _odd_mix(y_ssm, pr, ssm_w_glu[o].astype(BF16), conv_w[o].astype(F32), s, tm)
            wo = rc_w_out[o]
        h = _post(h, y1, y2, wo.astype(BF16), norm_mlp[i], w_mlp_in[i].astype(BF16), w_mlp_out[i].astype(BF16),
                  norm_ple[i], p.reshape(depth * t, -1), i, w_ple_proj[i].astype(BF16), w_ple_gate[i].astype(BF16),
                  norm_final, i == depth - 1, tm)
    return h.reshape(b, s, d)
```

```python
import functools
import math

import jax
import jax.numpy as jnp
from jax import lax
from jax.experimental import pallas as pl
from jax.experimental.pallas import tpu as pltpu

F32 = jnp.float32
BF16 = jnp.bfloat16

EPS = 1e-6
PLE_DIM = 256
DIFF_DH = 64
DIFF_HEADS = 4
DIFF_WIDTH = 512
SWA_DH = 64
SWA_WIDTH = 512
DILATIONS = (16, 4, 1)
N_BACK = 128
DIL_TILE = N_BACK * 16
DIL_GROUP = 4
SSM_GROUP_CH = 16
SSM_STATE = 64
SSM_WIDTH = 512
SSM_GROUPS = 32
SSM_BUNDLES = 4
SSM_CHUNK = 16
SSM_TILE = 1024
CONV_WIDTH = 512
LANES = 128
NEG_BIG = -1e30
VMEM_LIMIT = 60 * 1024 * 1024


def _cparams(sem):
    return pltpu.CompilerParams(dimension_semantics=sem, vmem_limit_bytes=VMEM_LIMIT)


def _resident(shape):
    nd = len(shape)
    return pl.BlockSpec(shape, lambda *_: (0,) * nd, pipeline_mode=pl.Buffered(1))


def _rms(x, g, eps=EPS):
    return x * lax.rsqrt(jnp.mean(x * x, axis=-1, keepdims=True) + eps) * g


def _sigmoid(x):
    return 1.0 / (1.0 + jnp.exp(-x))


def _gelu_tanh(x):
    c = math.sqrt(2.0 / math.pi)
    return x * (0.5 * (1.0 + jnp.tanh(c * (x + 0.044715 * (x * x * x)))))


def _norm_matmul_kernel(h_ref, g_ref, w_ref, *o_refs, splits):
    xn = _rms(h_ref[...], g_ref[...]).astype(BF16)
    for o_ref, (c0, c1) in zip(o_refs, splits):
        o_ref[...] = jnp.dot(xn, w_ref[:, c0:c1], preferred_element_type=F32).astype(o_ref.dtype)


def _norm_matmul(h, g, w, splits, dtypes, tm):
    t, d = h.shape
    n = w.shape[1]
    return pl.pallas_call(
        functools.partial(_norm_matmul_kernel, splits=splits),
        grid=(t // tm,),
        in_specs=[pl.BlockSpec((tm, d), lambda i: (i, 0)), _resident((1, d)), _resident((d, n))],
        out_specs=[pl.BlockSpec((tm, c1 - c0), lambda i: (i, 0)) for (c0, c1) in splits],
        out_shape=[jax.ShapeDtypeStruct((t, c1 - c0), dt) for (c0, c1), dt in zip(splits, dtypes)],
        compiler_params=_cparams(("parallel",)),
        name="norm_matmul",
    )(h, g.reshape(1, d), w)


def _ssm_conv_inproj_kernel(h_ref, hprev_ref, g_ref, w_ref, cw_ref, u_ref, yd_ref, *, tiles_per_seq):
    w = CONV_WIDTH
    g = g_ref[...]
    proj = jnp.dot(_rms(h_ref[...], g).astype(BF16), w_ref[...], preferred_element_type=F32)
    u_ref[...] = proj[:, :SSM_WIDTH]
    gb = proj[:, SSM_WIDTH:SSM_WIDTH + w]
    z = proj[:, SSM_WIDTH + w:SSM_WIDTH + 2 * w] * proj[:, SSM_WIDTH + 2 * w:]
    prev = jnp.dot(_rms(hprev_ref[...], g).astype(BF16), w_ref[:, SSM_WIDTH + w:], preferred_element_type=F32)
    seq_start = (pl.program_id(0) % tiles_per_seq) == 0
    zh = jnp.where(seq_start, 0.0, prev[:, :w] * prev[:, w:])
    row = lax.broadcasted_iota(jnp.int32, z.shape, 0)
    z1 = jnp.where(row == 0, zh[7:8, :], pltpu.roll(z, 1, 0))
    z2 = jnp.where(row == 0, zh[6:7, :], jnp.where(row == 1, zh[7:8, :], pltpu.roll(z, 2, 0)))
    cw = cw_ref[...]
    yd_ref[...] = (gb * (cw[0:1, :] * z + cw[1:2, :] * z1 + cw[2:3, :] * z2)).astype(yd_ref.dtype)


def _ssm_conv_inproj(h, g, w, conv_w, seq, tm):
    t, d = h.shape
    halo = 8
    return pl.pallas_call(
        functools.partial(_ssm_conv_inproj_kernel, tiles_per_seq=seq // tm),
        grid=(t // tm,),
        in_specs=[pl.BlockSpec((tm, d), lambda i: (i, 0)),
                  pl.BlockSpec((halo, d), lambda i: (jnp.maximum(i * (tm // halo) - 1, 0), 0)),
                  _resident((1, d)), _resident(w.shape), _resident(conv_w.shape)],
        out_specs=[pl.BlockSpec((tm, SSM_WIDTH), lambda i: (i, 0)), pl.BlockSpec((tm, CONV_WIDTH), lambda i: (i, 0))],
        out_shape=[jax.ShapeDtypeStruct((t, SSM_WIDTH), F32), jax.ShapeDtypeStruct((t, CONV_WIDTH), BF16)],
        compiler_params=_cparams(("parallel",)),
        name="ssm_conv_inproj",
    )(h, h, g.reshape(1, d), w, conv_w)


def _post_kernel(h_ref, y1_ref, y2_ref, wo_ref, gm_ref, w1_ref, w2_ref, gp_ref, p_ref, wp_ref, wg_ref,
                 gf_ref, *rest, ff_chunk, final, glu):
    o_ref = rest[-1]
    k1 = y1_ref.shape[1]
    h = h_ref[...]
    y1 = y1_ref[...]
    if glu:
        y1 = _gelu_tanh(y1)
        y1 = (y1 * _sigmoid(jnp.dot(y1.astype(BF16), rest[0][...], preferred_element_type=F32))).astype(BF16)
    h = h + jnp.dot(y1, wo_ref[:k1, :], preferred_element_type=F32)
    h = h + jnp.dot(y2_ref[...], wo_ref[k1:, :], preferred_element_type=F32)
    hn = _rms(h, gm_ref[...]).astype(BF16)
    d_ff = w1_ref.shape[1]
    acc = jnp.zeros_like(h)
    for c in range(d_ff // ff_chunk):
        a = jnp.dot(hn, w1_ref[:, c * ff_chunk:(c + 1) * ff_chunk], preferred_element_type=F32)
        a = jnp.square(jnp.maximum(a, 0.0)).astype(BF16)
        acc = acc + jnp.dot(a, w2_ref[c * ff_chunk:(c + 1) * ff_chunk, :], preferred_element_type=F32)
    h = h + acc
    hn = _rms(h, gp_ref[...]).astype(BF16)
    gate = _sigmoid(jnp.dot(hn, wg_ref[...], preferred_element_type=F32))
    emb = jnp.dot(p_ref[...].astype(BF16), wp_ref[...], preferred_element_type=F32)
    h = h + emb * gate
    if final:
        h = _rms(h, gf_ref[...])
    o_ref[...] = h


def _post(h, y1, y2, wo, gm, w1, w2, gp, p, layer, wp, wg, gf, final, tm, w_glu=None):
    t, d = h.shape
    row = lambda width: pl.BlockSpec((tm, width), lambda i: (i, 0))
    p_row = pl.BlockSpec((tm, p.shape[1]), lambda i: (layer * (t // tm) + i, 0))
    glu = w_glu is not None
    return pl.pallas_call(
        functools.partial(_post_kernel, ff_chunk=1024, final=final, glu=glu),
        grid=(t // tm,),
        in_specs=[row(d), row(y1.shape[1]), row(y2.shape[1]), _resident(wo.shape), _resident((1, d)),
                  _resident(w1.shape), _resident(w2.shape), _resident((1, d)), p_row,
                  _resident(wp.shape), _resident(wg.shape), _resident((1, d))]
                 + ([_resident(w_glu.shape)] if glu else []),
        out_specs=row(d),
        out_shape=jax.ShapeDtypeStruct((t, d), F32),
        compiler_params=_cparams(("parallel",)),
        name="post_mixer",
    )(h, y1, y2, wo, gm.reshape(1, d), w1, w2, gp.reshape(1, d), p, wp, wg, gf.reshape(1, d),
      *([w_glu] if glu else []))


def _stack_heads(x, scale):
    lane = lax.broadcasted_iota(jnp.int32, x.shape, 1)
    zero = jnp.zeros_like(x)
    lo = jnp.where(lane < 64, x, zero)
    hi = jnp.where(lane >= 64, x, zero)
    return jnp.concatenate([lo, hi], axis=0) * scale


def _diff_attn_kernel(q_ref, k_ref, v_ref, lq1_ref, lk1_ref, lq2_ref, lk2_ref, sg_ref, bias_ref, o_ref,
                      m_s, l_s, acc_s, s_s, *, tq, bk, lam_init):
    qi = pl.program_id(2)
    heads = m_s.shape[0]
    lanes = lambda h: pl.ds(h * LANES, LANES)
    scale = jnp.asarray(DIFF_DH ** -0.5, BF16)
    qs = [_stack_heads(q_ref[:, lanes(h)], scale) for h in range(heads)]
    n_full = (qi * tq) // bk

    def scores(j, slot):
        keys = pl.ds(pl.multiple_of(j * bk, bk), bk)
        for h in range(heads):
            s_s[slot, h] = lax.dot_general(qs[h], k_ref[keys, lanes(h)], (((1,), (1,)), ((), ())),
                                           preferred_element_type=F32)

    def update(j, slot, masked):
        keys = pl.ds(pl.multiple_of(j * bk, bk), bk)
        for h in range(heads):
            s = s_s[slot, h]
            if masked:
                s = s + bias_ref[(qi * tq - j * bk) // tq]
            m = m_s[h]
            m_new = jnp.maximum(m, jnp.max(s, axis=-1, keepdims=True))
            alpha = jnp.exp(m - m_new)
            p = jnp.exp(s - jnp.concatenate([m_new] * (bk // LANES), axis=1))
            m_s[h] = m_new
            l_s[h] = alpha * l_s[h] + sum(p[:, c * LANES:(c + 1) * LANES] for c in range(bk // LANES))
            acc_s[h] = alpha * acc_s[h] + jnp.dot(p.astype(BF16), v_ref[keys, lanes(h)],
                                                  preferred_element_type=F32)

    m_s[...] = jnp.full(m_s.shape, NEG_BIG, F32)
    l_s[...] = jnp.zeros(l_s.shape, F32)
    acc_s[...] = jnp.zeros(acc_s.shape, F32)

    scores(0, 0)

    def pair(i, carry):
        j = 2 * i
        scores(j + 1, 1)
        update(j, 0, False)
        scores(j + 2, 0)
        update(j + 1, 1, False)
        return carry

    lax.fori_loop(0, n_full // 2, pair, 0)
    odd = n_full % 2 == 1

    @pl.when(odd)
    def _():
        scores(n_full, 1)
        update(n_full - 1, 0, False)
        update(n_full, 1, True)

    @pl.when(jnp.logical_not(odd))
    def _():
        update(n_full, 0, True)
    lam = (jnp.exp(jnp.sum(lq1_ref[...] * lk1_ref[...], keepdims=True))
           - jnp.exp(jnp.sum(lq2_ref[...] * lk2_ref[...], keepdims=True)) + lam_init)
    for h in range(heads):
        o = acc_s[h] / jnp.sum(l_s[h], axis=-1, keepdims=True)
        o = o[:tq] - lam * o[tq:]
        o = o * lax.rsqrt(jnp.mean(o * o, axis=-1, keepdims=True) + 1e-5) * sg_ref[...]
        o_ref[:, lanes(h)] = (o * (1.0 - lam_init)).astype(o_ref.dtype)


def _diff_attn(proj, lq1, lk1, lq2, lk2, sub_gain, lam_init, tq, bk, heads):
    b, s, _ = proj.shape
    groups, width = DIFF_HEADS // heads, heads * LANES
    vec = lambda a: a.astype(F32).reshape(1, -1)
    off = jnp.arange(bk // tq)[:, None, None] * tq
    qpos = off + (jnp.arange(2 * tq) % tq)[None, :, None]
    bias = jnp.where(jnp.arange(bk)[None, None, :] <= qpos, 0.0, NEG_BIG).astype(F32)
    return pl.pallas_call(
        functools.partial(_diff_attn_kernel, tq=tq, bk=bk, lam_init=lam_init),
        grid=(b, groups, s // tq),
        in_specs=[pl.BlockSpec((None, tq, width), lambda bi, h, qi: (bi, qi, h)),
                  pl.BlockSpec((None, s, width), lambda bi, h, qi: (bi, 0, groups + h)),
                  pl.BlockSpec((None, s, width), lambda bi, h, qi: (bi, 0, 2 * groups + h)),
                  _resident((1, DIFF_DH)), _resident((1, DIFF_DH)), _resident((1, DIFF_DH)),
                  _resident((1, DIFF_DH)), _resident((1, 2 * DIFF_DH)), _resident(bias.shape)],
        out_specs=pl.BlockSpec((None, tq, width), lambda bi, h, qi: (bi, qi, h)),
        out_shape=jax.ShapeDtypeStruct((b, s, DIFF_WIDTH), BF16),
        scratch_shapes=[pltpu.VMEM((heads, 2 * tq, LANES), F32)] * 3 + [pltpu.VMEM((2, heads, 2 * tq, bk), F32)],
        compiler_params=_cparams(("parallel", "parallel", "arbitrary")),
        name="diff_attn",
    )(proj, proj, proj, vec(lq1), vec(lk1), vec(lq2), vec(lk2), vec(sub_gain), bias)


def _dil_attn_kernel(q_ref, kp_ref, kc_ref, vp_ref, vc_ref, o_ref, k_s, v_s, acc_s, m_s, l_s, bias_s, s_s):
    first = pl.program_id(2) == 0
    tile = q_ref.shape[0]
    k_s[pl.ds(0, tile), :] = kp_ref[...]
    k_s[pl.ds(tile, tile), :] = kc_ref[...]
    v_s[pl.ds(0, tile), :] = vp_ref[...]
    v_s[pl.ds(tile, tile), :] = vc_ref[...]
    scale = jnp.asarray(SWA_DH ** -0.5, BF16)
    lane = lax.broadcasted_iota(jnp.int32, (N_BACK, LANES), 1)
    lo = lane < 64
    r = lax.broadcasted_iota(jnp.int32, (2 * N_BACK, 2 * N_BACK), 0)
    qidx = jnp.where(r >= N_BACK, r - N_BACK, r)
    kk = lax.broadcasted_iota(jnp.int32, (2 * N_BACK, 2 * N_BACK), 1)
    band = (kk >= qidx) & (kk <= qidx + N_BACK)
    bias_s[0] = jnp.where(band, 0.0, NEG_BIG)
    bias_s[1] = jnp.where(band & jnp.logical_or(jnp.logical_not(first), kk >= N_BACK), 0.0, NEG_BIG)

    def windows(dil, n):
        st = (n % dil) + (n // dil) * (N_BACK * dil)
        kbase = tile + st - N_BACK * dil
        if dil > 1:
            return pl.ds(st, N_BACK, stride=dil), pl.ds(kbase, 2 * N_BACK, stride=dil)
        return pl.ds(st, N_BACK), pl.ds(kbase, 2 * N_BACK)

    def score(dil, n, slot):
        rows, krows = windows(dil, n)
        qs = _stack_heads(q_ref[rows, :].astype(BF16), scale)
        kb = k_s[krows, :].astype(BF16)
        s_s[slot] = lax.dot_general(qs, kb, (((1,), (1,)), ((), ())), preferred_element_type=F32)

    def finish(dil, n, slot):
        rows, krows = windows(dil, n)
        vb = v_s[krows, :].astype(BF16)
        s = s_s[slot] + bias_s[1 if n < dil else 0]
        m2 = jnp.max(s, axis=-1, keepdims=True)
        p = jnp.exp(s - m2)
        l2 = jnp.sum(p, axis=-1, keepdims=True)
        pv = jnp.dot(p.astype(BF16), vb, preferred_element_type=F32)
        acc_c = jnp.where(lo, pv[:N_BACK], pv[N_BACK:])
        m_c = jnp.where(lo, m2[:N_BACK], m2[N_BACK:])
        l_c = jnp.where(lo, l2[:N_BACK], l2[N_BACK:])
        if dil == DILATIONS[0]:
            acc_s[rows, :] = acc_c
            m_s[rows, :] = m_c
            l_s[rows, :] = l_c
        else:
            m_o = m_s[rows, :]
            m_n = jnp.maximum(m_o, m_c)
            a_o = jnp.exp(m_o - m_n)
            a_c = jnp.exp(m_c - m_n)
            acc_n = a_o * acc_s[rows, :] + a_c * acc_c
            l_n = a_o * l_s[rows, :] + a_c * l_c
            if dil == DILATIONS[-1]:
                o_ref[rows, :] = (acc_n / l_n).astype(o_ref.dtype)
            else:
                acc_s[rows, :] = acc_n
                m_s[rows, :] = m_n
                l_s[rows, :] = l_n

    groups = [(dil, g) for dil in DILATIONS for g in range(DIL_TILE // N_BACK // DIL_GROUP)]

    def score_group(i):
        dil, g = groups[i]
        for e in range(DIL_GROUP):
            score(dil, g * DIL_GROUP + e, (i % 2) * DIL_GROUP + e)

    score_group(0)
    for i, (dil, g) in enumerate(groups):
        if i + 1 < len(groups):
            score_group(i + 1)
        for e in range(DIL_GROUP):
            finish(dil, g * DIL_GROUP + e, (i % 2) * DIL_GROUP + e)


def _dil_attn(proj):
    b, s, _ = proj.shape
    pairs = SWA_WIDTH // LANES
    cur = lambda part: pl.BlockSpec((None, DIL_TILE, LANES), lambda bi, hp, i: (bi, i, part * pairs + hp))
    prev = lambda part: pl.BlockSpec((None, DIL_TILE, LANES),
                                     lambda bi, hp, i: (bi, jnp.maximum(i - 1, 0), part * pairs + hp))
    return pl.pallas_call(
        _dil_attn_kernel,
        grid=(b, pairs, s // DIL_TILE),
        in_specs=[cur(0), prev(1), cur(1), prev(2), cur(2)],
        out_specs=pl.BlockSpec((None, DIL_TILE, LANES), lambda bi, hp, i: (bi, i, hp)),
        out_shape=jax.ShapeDtypeStruct((b, s, SWA_WIDTH), BF16),
        scratch_shapes=[pltpu.VMEM((2 * DIL_TILE, LANES), F32), pltpu.VMEM((2 * DIL_TILE, LANES), F32),
                        pltpu.VMEM((DIL_TILE, LANES), F32), pltpu.VMEM((DIL_TILE, LANES), F32),
                        pltpu.VMEM((DIL_TILE, LANES), F32), pltpu.VMEM((2, 2 * N_BACK, 2 * N_BACK), F32),
                        pltpu.VMEM((2 * DIL_GROUP, 2 * N_BACK, 2 * N_BACK), F32)],
        compiler_params=_cparams(("parallel", "parallel", "arbitrary")),
        name="dilated_attn",
    )(proj, proj, proj, proj, proj)


def _bundle_block_diag(x):
    nb, gb = SSM_BUNDLES, SSM_GROUPS // SSM_BUNDLES
    _, c, p = x.shape
    x = x.reshape(nb, gb, c, p)
    out = jnp.zeros((nb, gb, c, gb, p), x.dtype)
    for g in range(gb):
        out = out.at[:, g, :, g, :].set(x[:, g])
    return out.reshape(nb, gb * c, gb * p)


def _s5_prep_kernel(bb_re_ref, bb_im_ref, c_re_ref, c_im_ref, pw_re_ref, pw_im_ref, toep_ref, ts_ref, fs_ref):
    lc = SSM_CHUNK
    hp = lax.Precision.HIGHEST
    nt = (((1,), (1,)), ((), ()))
    bb_re, bb_im, c_re, c_im = bb_re_ref[...], bb_im_ref[...], c_re_ref[...], c_im_ref[...]
    half = bb_re.shape[1]
    blk = lambda i: pl.ds(i * LANES, LANES)
    zero = jnp.zeros((LANES, LANES), toep_ref.dtype)
    for j in range(lc):
        a_re, a_im = pw_re_ref[j:j + 1, :], pw_im_ref[j:j + 1, :]
        ab_re = bb_re * a_re - bb_im * a_im
        ab_im = bb_re * a_im + bb_im * a_re
        lag = (lax.dot_general(ab_re, c_re, nt, precision=hp, preferred_element_type=F32)
               - lax.dot_general(ab_im, c_im, nt, precision=hp, preferred_element_type=F32))
        lag = lag.astype(toep_ref.dtype)
        for s in range(lc - j):
            toep_ref[blk(s), blk(s + j)] = lag
        for t in range(j):
            toep_ref[blk(j), blk(t)] = zero
        s = lc - 1 - j
        ts_ref[blk(s), pl.ds(0, half)] = ab_re.astype(ts_ref.dtype)
        ts_ref[blk(s), pl.ds(half, half)] = ab_im.astype(ts_ref.dtype)
    for t in range(lc):
        a_re, a_im = pw_re_ref[t + 1:t + 2, :], pw_im_ref[t + 1:t + 2, :]
        ca_re = c_re * a_re - c_im * a_im
        ca_im = c_re * a_im + c_im * a_re
        fs_ref[pl.ds(0, half), blk(t)] = ca_re.T.astype(fs_ref.dtype)
        fs_ref[pl.ds(half, half), blk(t)] = (-ca_im).T.astype(fs_ref.dtype)


def _s5_operators(lam_re, lam_im, log_dt, b_re, b_im, c_re, c_im, d_skip):
    lc = SSM_CHUNK
    nb, gb, st = SSM_BUNDLES, SSM_GROUPS // SSM_BUNDLES, SSM_STATE
    lr, li = lam_re.astype(F32), lam_im.astype(F32)
    dt = jnp.exp(log_dt.astype(F32))[:, None]
    mag = jnp.exp(lr * dt)
    abar_re, abar_im = mag * jnp.cos(li * dt), mag * jnp.sin(li * dt)
    den = lr * lr + li * li
    nr, ni = abar_re - 1.0, abar_im
    coef_re = (nr * lr + ni * li) / den
    coef_im = (ni * lr - nr * li) / den
    br, bi = b_re.astype(F32), b_im.astype(F32)
    bb_re = coef_re[..., None] * br - coef_im[..., None] * bi
    bb_im = coef_re[..., None] * bi + coef_im[..., None] * br
    j = jnp.arange(lc + 1, dtype=F32)[:, None, None]
    pmag = jnp.exp(j * (lr * dt)[None])
    pw_re, pw_im = pmag * jnp.cos(j * (li * dt)[None]), pmag * jnp.sin(j * (li * dt)[None])
    per_bundle = lambda pw: jnp.transpose(pw.reshape(lc + 1, nb, gb * st), (1, 0, 2))
    pw_re, pw_im = per_bundle(pw_re), per_bundle(pw_im)
    small = [_bundle_block_diag(jnp.swapaxes(bb_re, 1, 2)), _bundle_block_diag(jnp.swapaxes(bb_im, 1, 2)),
             _bundle_block_diag(c_re.astype(F32)), _bundle_block_diag(c_im.astype(F32)), pw_re, pw_im]
    width, states = lc * LANES, 2 * gb * st
    bundle = lambda shape: pl.BlockSpec((None,) + shape, lambda bi: (bi, 0, 0))
    toep, to_state, from_state = pl.pallas_call(
        _s5_prep_kernel,
        grid=(nb,),
        in_specs=[bundle(a.shape[1:]) for a in small],
        out_specs=[bundle((width, width)), bundle((width, states)), bundle((states, width))],
        out_shape=[jax.ShapeDtypeStruct((nb, width, width), BF16), jax.ShapeDtypeStruct((nb, width, states), BF16),
                   jax.ShapeDtypeStruct((nb, states, width), BF16)],
        compiler_params=_cparams(("parallel",)),
        name="s5_prep",
    )(*small)
    a_lc_re, a_lc_im = pw_re[:, lc], pw_im[:, lc]
    dec_a = jnp.concatenate([a_lc_re, a_lc_re], axis=-1)[:, None, :]
    dec_b = jnp.concatenate([-a_lc_im, a_lc_im], axis=-1)[:, None, :]
    skip = d_skip.astype(F32).reshape(nb, 1, LANES)
    return toep, to_state, from_state, dec_a, dec_b, skip


def _s5_kernel(u_ref, toep_ref, ts_ref, fs_ref, da_ref, db_ref, skip_ref, y_ref, lhs_s, ug_s, loc_s, cin_s,
               state_s):
    lc = SSM_CHUNK
    batch, tile, _ = u_ref.shape
    chunks = tile // lc
    slabs = state_s.shape[0]

    for b in range(batch):
        for s in range(lc):
            piece = u_ref[b, pl.ds(s, chunks, stride=lc), :]
            ug_s[s, pl.ds(b * chunks, chunks), :] = piece
            lhs_s[pl.ds(b * chunks, chunks), pl.ds(s * LANES, LANES)] = piece.astype(BF16)
    lhs = lhs_s[...]
    y = jnp.dot(lhs, toep_ref[...], preferred_element_type=F32)
    loc = jnp.dot(lhs, ts_ref[...], preferred_element_type=F32)
    for k in range(slabs):
        loc_s[k] = loc[:, k * LANES:(k + 1) * LANES]

    @pl.when(pl.program_id(1) == 0)
    def _():
        state_s[...] = jnp.zeros(state_s.shape, F32)

    da = [da_ref[:, k * LANES:(k + 1) * LANES] for k in range(slabs)]
    db = [db_ref[:, k * LANES:(k + 1) * LANES] for k in range(slabs)]

    def step(n, x):
        rows = pl.ds(n, batch, stride=chunks)
        for k in range(slabs):
            cin_s[k, rows, :] = x[k]
        return tuple(da[k] * x[k] + db[k] * x[(k + slabs // 2) % slabs] + loc_s[k, rows, :] for k in range(slabs))

    x = lax.fori_loop(0, chunks, step, tuple(state_s[k] for k in range(slabs)))
    for k in range(slabs):
        state_s[k] = x[k]
    cin = jnp.concatenate([cin_s[k] for k in range(slabs)], axis=1).astype(BF16)
    y = y + jnp.dot(cin, fs_ref[...], preferred_element_type=F32)
    skip = skip_ref[...]
    for b in range(batch):
        for t in range(lc):
            rows = pl.ds(b * chunks, chunks)
            y_ref[b, pl.ds(t, chunks, stride=lc), :] = (y[b * chunks:(b + 1) * chunks, t * LANES:(t + 1) * LANES]
                                                        + ug_s[t, rows, :] * skip)


def _s5_scan(u, ops, tile):
    toep, to_state, from_state, dec_a, dec_b, skip = ops
    batch, seq, _ = u.shape
    lc = SSM_CHUNK
    rows = batch * (tile // lc)
    slabs = to_state.shape[2] // LANES
    act = pl.BlockSpec((batch, tile, LANES), lambda bi, si: (0, si, bi))
    per_bundle = lambda a, **kw: pl.BlockSpec((None,) + a.shape[1:], lambda bi, si: (bi, 0, 0), **kw)
    once = dict(pipeline_mode=pl.Buffered(1))
    return pl.pallas_call(
        _s5_kernel,
        grid=(SSM_BUNDLES, seq // tile),
        in_specs=[act, per_bundle(toep, **once), per_bundle(to_state, **once), per_bundle(from_state, **once),
                  per_bundle(dec_a), per_bundle(dec_b), per_bundle(skip)],
        out_specs=act,
        out_shape=jax.ShapeDtypeStruct(u.shape, F32),
        scratch_shapes=[pltpu.VMEM((rows, lc * LANES), BF16), pltpu.VMEM((lc, rows, LANES), F32),
                        pltpu.VMEM((slabs, rows, LANES), F32), pltpu.VMEM((slabs, rows, LANES), F32),
                        pltpu.VMEM((slabs, batch, LANES), F32)],
        compiler_params=_cparams(("parallel", "arbitrary")),
        name="s5_scan",
    )(u, toep, to_state, from_state, dec_a, dec_b, skip)


def kernel(x, p, norm_mix, norm_mlp, norm_ple, w_mlp_in, w_mlp_out, w_ple_proj, w_ple_gate, attn_w_in, attn_w_out, diff_lq1, diff_lk1, diff_lq2, diff_lk2, diff_sub_gain, rc_w_in, rc_w_out, ssm_lambda_re, ssm_lambda_im, ssm_log_dt, ssm_b_re, ssm_b_im, ssm_c_re, ssm_c_im, ssm_d, ssm_w_glu, conv_w, norm_final):
    b, s, d = x.shape
    depth = p.shape[0]
    t = b * s
    tm = 512
    assert s % DIL_TILE == 0 and s % tm == 0 and d == 2 * DIFF_WIDTH
    h = x.reshape(t, d)
    for i in range(depth):
        if i % 2 == 0:
            e = i // 2
            lam_init = 0.8 - 0.6 * math.exp(-0.3 * i)
            pa, pb = _norm_matmul(h, norm_mix[i], attn_w_in[e].astype(BF16),
                                  ((0, 3 * DIFF_WIDTH), (3 * DIFF_WIDTH, 3 * DIFF_WIDTH + 3 * SWA_WIDTH)),
                                  (BF16, F32), tm)
            y1 = _diff_attn(pa.reshape(b, s, -1), diff_lq1[e], diff_lk1[e], diff_lq2[e], diff_lk2[e],
                            diff_sub_gain[e], lam_init, tq=256, bk=512, heads=2).reshape(t, -1)
            y2 = _dil_attn(pb.reshape(b, s, -1)).reshape(t, -1)
            wo, w_glu = attn_w_out[e], None
        else:
            o = i // 2
            u, y2 = _ssm_conv_inproj(h, norm_mix[i], rc_w_in[o].astype(BF16), conv_w[o].astype(F32), s, tm)
            ops = _s5_operators(ssm_lambda_re[o], ssm_lambda_im[o], ssm_log_dt[o], ssm_b_re[o], ssm_b_im[o],
                                ssm_c_re[o], ssm_c_im[o], ssm_d[o])
            y1 = _s5_scan(u.reshape(b, s, -1), ops, min(SSM_TILE, s)).reshape(t, -1)
            wo, w_glu = rc_w_out[o], ssm_w_glu[o].astype(BF16)
        h = _post(h, y1, y2, wo.astype(BF16), norm_mlp[i], w_mlp_in[i].astype(BF16), w_mlp_out[i].astype(BF16),
                  norm_ple[i], p.reshape(depth * t, -1), i, w_ple_proj[i].astype(BF16), w_ple_gate[i].astype(BF16),
                  norm_final, i == depth - 1, tm, w_glu)
    return h.reshape(b, s, d)
```

```python
import functools
import math

import jax
import jax.numpy as jnp
from jax import lax
from jax.experimental import pallas as pl
from jax.experimental.pallas import tpu as pltpu

F32 = jnp.float32
BF16 = jnp.bfloat16

EPS = 1e-6
PLE_DIM = 256
DIFF_DH = 64
DIFF_HEADS = 4
DIFF_WIDTH = 512
SWA_DH = 64
SWA_WIDTH = 512
DILATIONS = (16, 4, 1)
N_BACK = 128
DIL_TILE = N_BACK * 16
DIL_GROUP = 2
SSM_GROUP_CH = 16
SSM_STATE = 64
SSM_WIDTH = 512
SSM_GROUPS = 32
SSM_BUNDLES = 4
SSM_CHUNK = 16
SSM_TILE = 1024
CONV_WIDTH = 512
LANES = 128
NEG_BIG = -1e30
VMEM_LIMIT = 60 * 1024 * 1024


def _cparams(sem):
    return pltpu.CompilerParams(dimension_semantics=sem, vmem_limit_bytes=VMEM_LIMIT)


def _resident(shape):
    nd = len(shape)
    return pl.BlockSpec(shape, lambda *_: (0,) * nd, pipeline_mode=pl.Buffered(1))


def _rms(x, g, eps=EPS):
    return x * lax.rsqrt(jnp.mean(x * x, axis=-1, keepdims=True) + eps) * g


def _sigmoid(x):
    return 1.0 / (1.0 + jnp.exp(-x))


def _gelu_tanh(x):
    c = math.sqrt(2.0 / math.pi)
    return x * (0.5 * (1.0 + jnp.tanh(c * (x + 0.044715 * (x * x * x)))))


def _norm_matmul_kernel(h_ref, g_ref, w_ref, *o_refs, splits):
    xn = _rms(h_ref[...], g_ref[...]).astype(BF16)
    for o_ref, (c0, c1) in zip(o_refs, splits):
        o_ref[...] = jnp.dot(xn, w_ref[:, c0:c1], preferred_element_type=F32).astype(o_ref.dtype)


def _norm_matmul(h, g, w, splits, dtypes, tm):
    t, d = h.shape
    n = w.shape[1]
    return pl.pallas_call(
        functools.partial(_norm_matmul_kernel, splits=splits),
        grid=(t // tm,),
        in_specs=[pl.BlockSpec((tm, d), lambda i: (i, 0)), _resident((1, d)), _resident((d, n))],
        out_specs=[pl.BlockSpec((tm, c1 - c0), lambda i: (i, 0)) for (c0, c1) in splits],
        out_shape=[jax.ShapeDtypeStruct((t, c1 - c0), dt) for (c0, c1), dt in zip(splits, dtypes)],
        compiler_params=_cparams(("parallel",)),
        name="norm_matmul",
    )(h, g.reshape(1, d), w)


def _ssm_conv_inproj_kernel(h_ref, hprev_ref, g_ref, w_ref, cw_ref, u_ref, yd_ref, *, tiles_per_seq):
    w = CONV_WIDTH
    g = g_ref[...]
    proj = jnp.dot(_rms(h_ref[...], g).astype(BF16), w_ref[...], preferred_element_type=F32)
    u_ref[...] = proj[:, :SSM_WIDTH]
    gb = proj[:, SSM_WIDTH:SSM_WIDTH + w]
    z = proj[:, SSM_WIDTH + w:SSM_WIDTH + 2 * w] * proj[:, SSM_WIDTH + 2 * w:]
    prev = jnp.dot(_rms(hprev_ref[...], g).astype(BF16), w_ref[:, SSM_WIDTH + w:], preferred_element_type=F32)
    seq_start = (pl.program_id(0) % tiles_per_seq) == 0
    zh = jnp.where(seq_start, 0.0, prev[:, :w] * prev[:, w:])
    row = lax.broadcasted_iota(jnp.int32, z.shape, 0)
    z1 = jnp.where(row == 0, zh[7:8, :], pltpu.roll(z, 1, 0))
    z2 = jnp.where(row == 0, zh[6:7, :], jnp.where(row == 1, zh[7:8, :], pltpu.roll(z, 2, 0)))
    cw = cw_ref[...]
    yd_ref[...] = (gb * (cw[0:1, :] * z + cw[1:2, :] * z1 + cw[2:3, :] * z2)).astype(yd_ref.dtype)


def _ssm_conv_inproj(h, g, w, conv_w, seq, tm):
    t, d = h.shape
    halo = 8
    return pl.pallas_call(
        functools.partial(_ssm_conv_inproj_kernel, tiles_per_seq=seq // tm),
        grid=(t // tm,),
        in_specs=[pl.BlockSpec((tm, d), lambda i: (i, 0)),
                  pl.BlockSpec((halo, d), lambda i: (jnp.maximum(i * (tm // halo) - 1, 0), 0)),
                  _resident((1, d)), _resident(w.shape), _resident(conv_w.shape)],
        out_specs=[pl.BlockSpec((tm, SSM_WIDTH), lambda i: (i, 0)), pl.BlockSpec((tm, CONV_WIDTH), lambda i: (i, 0))],
        out_shape=[jax.ShapeDtypeStruct((t, SSM_WIDTH), F32), jax.ShapeDtypeStruct((t, CONV_WIDTH), BF16)],
        compiler_params=_cparams(("parallel",)),
        name="ssm_conv_inproj",
    )(h, h, g.reshape(1, d), w, conv_w)


def _post_kernel(h_ref, y1_ref, y2_ref, wo_ref, gm_ref, w1_ref, w2_ref, gp_ref, p_ref, wp_ref, wg_ref,
                 gf_ref, *rest, ff_chunk, final, glu):
    o_ref = rest[-1]
    k1 = y1_ref.shape[1]
    h = h_ref[...]
    y1 = y1_ref[...]
    if glu:
        y1 = _gelu_tanh(y1)
        y1 = (y1 * _sigmoid(jnp.dot(y1.astype(BF16), rest[0][...], preferred_element_type=F32))).astype(BF16)
    h = h + jnp.dot(y1, wo_ref[:k1, :], preferred_element_type=F32)
    h = h + jnp.dot(y2_ref[...], wo_ref[k1:, :], preferred_element_type=F32)
    hn = _rms(h, gm_ref[...]).astype(BF16)
    d_ff = w1_ref.shape[1]
    acc = jnp.zeros_like(h)
    for c in range(d_ff // ff_chunk):
        a = jnp.dot(hn, w1_ref[:, c * ff_chunk:(c + 1) * ff_chunk], preferred_element_type=F32)
        a = jnp.square(jnp.maximum(a, 0.0)).astype(BF16)
        acc = acc + jnp.dot(a, w2_ref[c * ff_chunk:(c + 1) * ff_chunk, :], preferred_element_type=F32)
    h = h + acc
    hn = _rms(h, gp_ref[...]).astype(BF16)
    gate = _sigmoid(jnp.dot(hn, wg_ref[...], preferred_element_type=F32))
    emb = jnp.dot(p_ref[...].astype(BF16), wp_ref[...], preferred_element_type=F32)
    h = h + emb * gate
    if final:
        h = _rms(h, gf_ref[...])
    o_ref[...] = h


def _post(h, y1, y2, wo, gm, w1, w2, gp, p, layer, wp, wg, gf, final, tm, w_glu=None):
    t, d = h.shape
    row = lambda width: pl.BlockSpec((tm, width), lambda i: (i, 0))
    p_row = pl.BlockSpec((tm, p.shape[1]), lambda i: (layer * (t // tm) + i, 0))
    glu = w_glu is not None
    return pl.pallas_call(
        functools.partial(_post_kernel, ff_chunk=1024, final=final, glu=glu),
        grid=(t // tm,),
        in_specs=[row(d), row(y1.shape[1]), row(y2.shape[1]), _resident(wo.shape), _resident((1, d)),
                  _resident(w1.shape), _resident(w2.shape), _resident((1, d)), p_row,
                  _resident(wp.shape), _resident(wg.shape), _resident((1, d))]
                 + ([_resident(w_glu.shape)] if glu else []),
        out_specs=row(d),
        out_shape=jax.ShapeDtypeStruct((t, d), F32),
        compiler_params=_cparams(("parallel",)),
        name="post_mixer",
    )(h, y1, y2, wo, gm.reshape(1, d), w1, w2, gp.reshape(1, d), p, wp, wg, gf.reshape(1, d),
      *([w_glu] if glu else []))


def _stack_heads(x, scale):
    lane = lax.broadcasted_iota(jnp.int32, x.shape, 1)
    zero = jnp.zeros_like(x)
    lo = jnp.where(lane < 64, x, zero)
    hi = jnp.where(lane >= 64, x, zero)
    return jnp.concatenate([lo, hi], axis=0) * scale


def _diff_attn_kernel(q_ref, k_ref, v_ref, lq1_ref, lk1_ref, lq2_ref, lk2_ref, sg_ref, bias_ref, o_ref,
                      m_s, l_s, acc_s, s_s, *, tq, bk, lam_init):
    qi = pl.program_id(2)
    heads = m_s.shape[0]
    lanes = lambda h: pl.ds(h * LANES, LANES)
    scale = jnp.asarray(DIFF_DH ** -0.5, BF16)
    qs = [_stack_heads(q_ref[:, lanes(h)], scale) for h in range(heads)]
    n_full = (qi * tq) // bk

    def scores(j, slot):
        keys = pl.ds(pl.multiple_of(j * bk, bk), bk)
        for h in range(heads):
            s_s[slot, h] = lax.dot_general(qs[h], k_ref[keys, lanes(h)], (((1,), (1,)), ((), ())),
                                           preferred_element_type=F32)

    def update(j, slot, masked):
        keys = pl.ds(pl.multiple_of(j * bk, bk), bk)
        for h in range(heads):
            s = s_s[slot, h]
            if masked:
                s = s + bias_ref[(qi * tq - j * bk) // tq]
            m = m_s[h]
            m_new = jnp.maximum(m, jnp.max(s, axis=-1, keepdims=True))
            alpha = jnp.exp(m - m_new)
            p = jnp.exp(s - jnp.concatenate([m_new] * (bk // LANES), axis=1))
            m_s[h] = m_new
            l_s[h] = alpha * l_s[h] + sum(p[:, c * LANES:(c + 1) * LANES] for c in range(bk // LANES))
            acc_s[h] = alpha * acc_s[h] + jnp.dot(p.astype(BF16), v_ref[keys, lanes(h)],
                                                  preferred_element_type=F32)

    m_s[...] = jnp.full(m_s.shape, NEG_BIG, F32)
    l_s[...] = jnp.zeros(l_s.shape, F32)
    acc_s[...] = jnp.zeros(acc_s.shape, F32)

    scores(0, 0)

    def pair(i, carry):
        j = 2 * i
        scores(j + 1, 1)
        update(j, 0, False)
        scores(j + 2, 0)
        update(j + 1, 1, False)
        return carry

    lax.fori_loop(0, n_full // 2, pair, 0)
    odd = n_full % 2 == 1

    @pl.when(odd)
    def _():
        scores(n_full, 1)
        update(n_full - 1, 0, False)
        update(n_full, 1, True)

    @pl.when(jnp.logical_not(odd))
    def _():
        update(n_full, 0, True)
    lam = (jnp.exp(jnp.sum(lq1_ref[...] * lk1_ref[...], keepdims=True))
           - jnp.exp(jnp.sum(lq2_ref[...] * lk2_ref[...], keepdims=True)) + lam_init)
    for h in range(heads):
        o = acc_s[h] / jnp.sum(l_s[h], axis=-1, keepdims=True)
        o = o[:tq] - lam * o[tq:]
        o = o * lax.rsqrt(jnp.mean(o * o, axis=-1, keepdims=True) + 1e-5) * sg_ref[...]
        o_ref[:, lanes(h)] = (o * (1.0 - lam_init)).astype(o_ref.dtype)


def _diff_attn(proj, lq1, lk1, lq2, lk2, sub_gain, lam_init, tq, bk, heads):
    b, s, _ = proj.shape
    groups, width = DIFF_HEADS // heads, heads * LANES
    vec = lambda a: a.astype(F32).reshape(1, -1)
    off = jnp.arange(bk // tq)[:, None, None] * tq
    qpos = off + (jnp.arange(2 * tq) % tq)[None, :, None]
    bias = jnp.where(jnp.arange(bk)[None, None, :] <= qpos, 0.0, NEG_BIG).astype(F32)
    return pl.pallas_call(
        functools.partial(_diff_attn_kernel, tq=tq, bk=bk, lam_init=lam_init),
        grid=(b, groups, s // tq),
        in_specs=[pl.BlockSpec((None, tq, width), lambda bi, h, qi: (bi, qi, h)),
                  pl.BlockSpec((None, s, width), lambda bi, h, qi: (bi, 0, groups + h)),
                  pl.BlockSpec((None, s, width), lambda bi, h, qi: (bi, 0, 2 * groups + h)),
                  _resident((1, DIFF_DH)), _resident((1, DIFF_DH)), _resident((1, DIFF_DH)),
                  _resident((1, DIFF_DH)), _resident((1, 2 * DIFF_DH)), _resident(bias.shape)],
        out_specs=pl.BlockSpec((None, tq, width), lambda bi, h, qi: (bi, qi, h)),
        out_shape=jax.ShapeDtypeStruct((b, s, DIFF_WIDTH), BF16),
        scratch_shapes=[pltpu.VMEM((heads, 2 * tq, LANES), F32)] * 3 + [pltpu.VMEM((2, heads, 2 * tq, bk), F32)],
        compiler_params=_cparams(("parallel", "parallel", "arbitrary")),
        name="diff_attn",
    )(proj, proj, proj, vec(lq1), vec(lk1), vec(lq2), vec(lk2), vec(sub_gain), bias)


def _dil_attn_kernel(q_ref, kp_ref, kc_ref, vp_ref, vc_ref, o_ref, k_s, v_s, acc_s, m_s, l_s, bias_s, s_s,
                     q4_s, k4_s, v4_s, acc4_s, m4_s, l4_s):
    first = pl.program_id(2) == 0
    tile = q_ref.shape[0]
    k_s[pl.ds(0, tile), :] = kp_ref[...]
    k_s[pl.ds(tile, tile), :] = kc_ref[...]
    v_s[pl.ds(0, tile), :] = vp_ref[...]
    v_s[pl.ds(tile, tile), :] = vc_ref[...]
    qt, kt = tile // 4, tile // 2
    for rho in range(4):
        q4_s[pl.ds(rho * qt, qt), :] = q_ref[pl.ds(rho, qt, stride=4), :]
        k4_s[pl.ds(rho * kt, qt), :] = kp_ref[pl.ds(rho, qt, stride=4), :]
        k4_s[pl.ds(rho * kt + qt, qt), :] = kc_ref[pl.ds(rho, qt, stride=4), :]
        v4_s[pl.ds(rho * kt, qt), :] = vp_ref[pl.ds(rho, qt, stride=4), :]
        v4_s[pl.ds(rho * kt + qt, qt), :] = vc_ref[pl.ds(rho, qt, stride=4), :]
    scale = jnp.asarray(SWA_DH ** -0.5, BF16)
    lane = lax.broadcasted_iota(jnp.int32, (N_BACK, LANES), 1)
    lo = lane < 64
    r = lax.broadcasted_iota(jnp.int32, (2 * N_BACK, 2 * N_BACK), 0)
    qidx = jnp.where(r >= N_BACK, r - N_BACK, r)
    kk = lax.broadcasted_iota(jnp.int32, (2 * N_BACK, 2 * N_BACK), 1)
    band = (kk >= qidx) & (kk <= qidx + N_BACK)
    bias_s[0] = jnp.where(band, 0.0, NEG_BIG)
    bias_s[1] = jnp.where(band & jnp.logical_or(jnp.logical_not(first), kk >= N_BACK), 0.0, NEG_BIG)

    def windows(dil, n):
        rho, rest = n % 4, n // 4
        if dil == 16:
            rows = pl.ds(rho * qt + rest, N_BACK, stride=4)
            return (q4_s, rows), (k4_s, v4_s, pl.ds(rho * kt + rest, 2 * N_BACK, stride=4)), None, (state4, rows)
        if dil == 4:
            rows = pl.ds(rho * qt + rest * N_BACK, N_BACK)
            keys = pl.ds(rho * kt + qt - N_BACK + rest * N_BACK, 2 * N_BACK)
            return ((q4_s, rows), (k4_s, v4_s, keys), (state4, rows),
                    (state, pl.ds(rho + rest * 4 * N_BACK, N_BACK, stride=4)))
        rows = pl.ds(n * N_BACK, N_BACK)
        return (q_ref, rows), (k_s, v_s, pl.ds(tile + (n - 1) * N_BACK, 2 * N_BACK)), (state, rows), None

    state, state4 = (acc_s, m_s, l_s), (acc4_s, m4_s, l4_s)

    def score(dil, n, slot):
        (q_buf, rows), (k_buf, _, krows), _, _ = windows(dil, n)
        qs = _stack_heads(q_buf[rows, :].astype(BF16), scale)
        kb = k_buf[krows, :].astype(BF16)
        s_s[slot] = lax.dot_general(qs, kb, (((1,), (1,)), ((), ())), preferred_element_type=F32)

    def finish(dil, n, slot):
        (_, rows), (_, v_buf, krows), old, new = windows(dil, n)
        vb = v_buf[krows, :].astype(BF16)
        s = s_s[slot] + bias_s[1 if n < dil else 0]
        m2 = jnp.max(s, axis=-1, keepdims=True)
        p = jnp.exp(s - m2)
        l2 = jnp.sum(p, axis=-1, keepdims=True)
        pv = jnp.dot(p.astype(BF16), vb, preferred_element_type=F32)
        acc_c = jnp.where(lo, pv[:N_BACK], pv[N_BACK:])
        m_c = jnp.where(lo, m2[:N_BACK], m2[N_BACK:])
        l_c = jnp.where(lo, l2[:N_BACK], l2[N_BACK:])
        if old is not None:
            (acc_o, m_o, l_o), orows = old
            m_o = m_o[orows, :]
            m_n = jnp.maximum(m_o, m_c)
            a_o = jnp.exp(m_o - m_n)
            a_c = jnp.exp(m_c - m_n)
            acc_c = a_o * acc_o[orows, :] + a_c * acc_c
            l_c = a_o * l_o[orows, :] + a_c * l_c
            m_c = m_n
        if new is None:
            o_ref[rows, :] = (acc_c / l_c).astype(o_ref.dtype)
        else:
            (acc_n, m_n, l_n), nrows = new
            acc_n[nrows, :] = acc_c
            m_n[nrows, :] = m_c
            l_n[nrows, :] = l_c

    groups = [(dil, g) for dil in DILATIONS for g in range(DIL_TILE // N_BACK // DIL_GROUP)]

    def score_group(i):
        dil, g = groups[i]
        for e in range(DIL_GROUP):
            score(dil, g * DIL_GROUP + e, (i % 2) * DIL_GROUP + e)

    score_group(0)
    for i, (dil, g) in enumerate(groups):
        if i + 1 < len(groups):
            score_group(i + 1)
        for e in range(DIL_GROUP):
            finish(dil, g * DIL_GROUP + e, (i % 2) * DIL_GROUP + e)


def _dil_attn(proj):
    b, s, _ = proj.shape
    pairs = SWA_WIDTH // LANES
    cur = lambda part: pl.BlockSpec((None, DIL_TILE, LANES), lambda bi, hp, i: (bi, i, part * pairs + hp))
    prev = lambda part: pl.BlockSpec((None, DIL_TILE, LANES),
                                     lambda bi, hp, i: (bi, jnp.maximum(i - 1, 0), part * pairs + hp))
    return pl.pallas_call(
        _dil_attn_kernel,
        grid=(b, pairs, s // DIL_TILE),
        in_specs=[cur(0), prev(1), cur(1), prev(2), cur(2)],
        out_specs=pl.BlockSpec((None, DIL_TILE, LANES), lambda bi, hp, i: (bi, i, hp)),
        out_shape=jax.ShapeDtypeStruct((b, s, SWA_WIDTH), BF16),
        scratch_shapes=[pltpu.VMEM((2 * DIL_TILE, LANES), F32), pltpu.VMEM((2 * DIL_TILE, LANES), F32),
                        pltpu.VMEM((DIL_TILE, LANES), F32), pltpu.VMEM((DIL_TILE, LANES), F32),
                        pltpu.VMEM((DIL_TILE, LANES), F32), pltpu.VMEM((2, 2 * N_BACK, 2 * N_BACK), F32),
                        pltpu.VMEM((2 * DIL_GROUP, 2 * N_BACK, 2 * N_BACK), F32),
                        pltpu.VMEM((DIL_TILE, LANES), F32), pltpu.VMEM((2 * DIL_TILE, LANES), F32),
                        pltpu.VMEM((2 * DIL_TILE, LANES), F32), pltpu.VMEM((DIL_TILE, LANES), F32),
                        pltpu.VMEM((DIL_TILE, LANES), F32), pltpu.VMEM((DIL_TILE, LANES), F32)],
        compiler_params=_cparams(("parallel", "parallel", "arbitrary")),
        name="dilated_attn",
    )(proj, proj, proj, proj, proj)


def _bundle_block_diag(x):
    nb, gb = SSM_BUNDLES, SSM_GROUPS // SSM_BUNDLES
    _, c, p = x.shape
    x = x.reshape(nb, gb, c, p)
    out = jnp.zeros((nb, gb, c, gb, p), x.dtype)
    for g in range(gb):
        out = out.at[:, g, :, g, :].set(x[:, g])
    return out.reshape(nb, gb * c, gb * p)


def _s5_prep_kernel(bb_re_ref, bb_im_ref, c_re_ref, c_im_ref, pw_re_ref, pw_im_ref, toep_ref, ts_ref, fs_ref):
    lc = SSM_CHUNK
    hp = lax.Precision.HIGHEST
    nt = (((1,), (1,)), ((), ()))
    bb_re, bb_im, c_re, c_im = bb_re_ref[...], bb_im_ref[...], c_re_ref[...], c_im_ref[...]
    half = bb_re.shape[1]
    blk = lambda i: pl.ds(i * LANES, LANES)
    zero = jnp.zeros((LANES, LANES), toep_ref.dtype)
    for j in range(lc):
        a_re, a_im = pw_re_ref[j:j + 1, :], pw_im_ref[j:j + 1, :]
        ab_re = bb_re * a_re - bb_im * a_im
        ab_im = bb_re * a_im + bb_im * a_re
        lag = (lax.dot_general(ab_re, c_re, nt, precision=hp, preferred_element_type=F32)
               - lax.dot_general(ab_im, c_im, nt, precision=hp, preferred_element_type=F32))
        lag = lag.astype(toep_ref.dtype)
        for s in range(lc - j):
            toep_ref[blk(s), blk(s + j)] = lag
        for t in range(j):
            toep_ref[blk(j), blk(t)] = zero
        s = lc - 1 - j
        ts_ref[blk(s), pl.ds(0, half)] = ab_re.astype(ts_ref.dtype)
        ts_ref[blk(s), pl.ds(half, half)] = ab_im.astype(ts_ref.dtype)
    for t in range(lc):
        a_re, a_im = pw_re_ref[t + 1:t + 2, :], pw_im_ref[t + 1:t + 2, :]
        ca_re = c_re * a_re - c_im * a_im
        ca_im = c_re * a_im + c_im * a_re
        fs_ref[pl.ds(0, half), blk(t)] = ca_re.T.astype(fs_ref.dtype)
        fs_ref[pl.ds(half, half), blk(t)] = (-ca_im).T.astype(fs_ref.dtype)


def _s5_operators(lam_re, lam_im, log_dt, b_re, b_im, c_re, c_im, d_skip):
    lc = SSM_CHUNK
    nb, gb, st = SSM_BUNDLES, SSM_GROUPS // SSM_BUNDLES, SSM_STATE
    lr, li = lam_re.astype(F32), lam_im.astype(F32)
    dt = jnp.exp(log_dt.astype(F32))[:, None]
    mag = jnp.exp(lr * dt)
    abar_re, abar_im = mag * jnp.cos(li * dt), mag * jnp.sin(li * dt)
    den = lr * lr + li * li
    nr, ni = abar_re - 1.0, abar_im
    coef_re = (nr * lr + ni * li) / den
    coef_im = (ni * lr - nr * li) / den
    br, bi = b_re.astype(F32), b_im.astype(F32)
    bb_re = coef_re[..., None] * br - coef_im[..., None] * bi
    bb_im = coef_re[..., None] * bi + coef_im[..., None] * br
    j = jnp.arange(lc + 1, dtype=F32)[:, None, None]
    pmag = jnp.exp(j * (lr * dt)[None])
    pw_re, pw_im = pmag * jnp.cos(j * (li * dt)[None]), pmag * jnp.sin(j * (li * dt)[None])
    per_bundle = lambda pw: jnp.transpose(pw.reshape(lc + 1, nb, gb * st), (1, 0, 2))
    pw_re, pw_im = per_bundle(pw_re), per_bundle(pw_im)
    small = [_bundle_block_diag(jnp.swapaxes(bb_re, 1, 2)), _bundle_block_diag(jnp.swapaxes(bb_im, 1, 2)),
             _bundle_block_diag(c_re.astype(F32)), _bundle_block_diag(c_im.astype(F32)), pw_re, pw_im]
    width, states = lc * LANES, 2 * gb * st
    bundle = lambda shape: pl.BlockSpec((None,) + shape, lambda bi: (bi, 0, 0))
    toep, to_state, from_state = pl.pallas_call(
        _s5_prep_kernel,
        grid=(nb,),
        in_specs=[bundle(a.shape[1:]) for a in small],
        out_specs=[bundle((width, width)), bundle((width, states)), bundle((states, width))],
        out_shape=[jax.ShapeDtypeStruct((nb, width, width), BF16), jax.ShapeDtypeStruct((nb, width, states), BF16),
                   jax.ShapeDtypeStruct((nb, states, width), BF16)],
        compiler_params=_cparams(("parallel",)),
        name="s5_prep",
    )(*small)
    a_lc_re, a_lc_im = pw_re[:, lc], pw_im[:, lc]
    dec_a = jnp.concatenate([a_lc_re, a_lc_re], axis=-1)[:, None, :]
    dec_b = jnp.concatenate([-a_lc_im, a_lc_im], axis=-1)[:, None, :]
    skip = d_skip.astype(F32).reshape(nb, 1, LANES)
    return toep, to_state, from_state, dec_a, dec_b, skip


def _s5_kernel(u_ref, toep_ref, ts_ref, fs_ref, da_ref, db_ref, skip_ref, y_ref, lhs_s, ug_s, loc_s, cin_s,
               state_s):
    lc = SSM_CHUNK
    batch, tile, _ = u_ref.shape
    chunks = tile // lc
    slabs = state_s.shape[0]

    for b in range(batch):
        for s in range(lc):
            piece = u_ref[b, pl.ds(s, chunks, stride=lc), :]
            ug_s[s, pl.ds(b * chunks, chunks), :] = piece
            lhs_s[pl.ds(b * chunks, chunks), pl.ds(s * LANES, LANES)] = piece.astype(BF16)
    lhs = lhs_s[...]
    y = jnp.dot(lhs, toep_ref[...], preferred_element_type=F32)
    loc = jnp.dot(lhs, ts_ref[...], preferred_element_type=F32)
    for k in range(slabs):
        loc_s[k] = loc[:, k * LANES:(k + 1) * LANES]

    @pl.when(pl.program_id(1) == 0)
    def _():
        state_s[...] = jnp.zeros(state_s.shape, F32)

    da = [da_ref[:, k * LANES:(k + 1) * LANES] for k in range(slabs)]
    db = [db_ref[:, k * LANES:(k + 1) * LANES] for k in range(slabs)]

    def step(n, x):
        rows = pl.ds(n, batch, stride=chunks)
        for k in range(slabs):
            cin_s[k, rows, :] = x[k]
        return tuple(da[k] * x[k] + db[k] * x[(k + slabs // 2) % slabs] + loc_s[k, rows, :] for k in range(slabs))

    x = lax.fori_loop(0, chunks, step, tuple(state_s[k] for k in range(slabs)))
    for k in range(slabs):
        state_s[k] = x[k]
    cin = jnp.concatenate([cin_s[k] for k in range(slabs)], axis=1).astype(BF16)
    y = y + jnp.dot(cin, fs_ref[...], preferred_element_type=F32)
    skip = skip_ref[...]
    for b in range(batch):
        for t in range(lc):
            rows = pl.ds(b * chunks, chunks)
            y_ref[b, pl.ds(t, chunks, stride=lc), :] = (y[b * chunks:(b + 1) * chunks, t * LANES:(t + 1) * LANES]
                                                        + ug_s[t, rows, :] * skip)


def _s5_scan(u, ops, tile):
    toep, to_state, from_state, dec_a, dec_b, skip = ops
    batch, seq, _ = u.shape
    lc = SSM_CHUNK
    rows = batch * (tile // lc)
    slabs = to_state.shape[2] // LANES
    act = pl.BlockSpec((batch, tile, LANES), lambda bi, si: (0, si, bi))
    per_bundle = lambda a, **kw: pl.BlockSpec((None,) + a.shape[1:], lambda bi, si: (bi, 0, 0), **kw)
    once = dict(pipeline_mode=pl.Buffered(1))
    return pl.pallas_call(
        _s5_kernel,
        grid=(SSM_BUNDLES, seq // tile),
        in_specs=[act, per_bundle(toep, **once), per_bundle(to_state, **once), per_bundle(from_state, **once),
                  per_bundle(dec_a), per_bundle(dec_b), per_bundle(skip)],
        out_specs=act,
        out_shape=jax.ShapeDtypeStruct(u.shape, F32),
        scratch_shapes=[pltpu.VMEM((rows, lc * LANES), BF16), pltpu.VMEM((lc, rows, LANES), F32),
                        pltpu.VMEM((slabs, rows, LANES), F32), pltpu.VMEM((slabs, rows, LANES), F32),
                        pltpu.VMEM((slabs, batch, LANES), F32)],
        compiler_params=_cparams(("parallel", "arbitrary")),
        name="s5_scan",
    )(u, toep, to_state, from_state, dec_a, dec_b, skip)


def kernel(x, p, norm_mix, norm_mlp, norm_ple, w_mlp_in, w_mlp_out, w_ple_proj, w_ple_gate, attn_w_in, attn_w_out, diff_lq1, diff_lk1, diff_lq2, diff_lk2, diff_sub_gain, rc_w_in, rc_w_out, ssm_lambda_re, ssm_lambda_im, ssm_log_dt, ssm_b_re, ssm_b_im, ssm_c_re, ssm_c_im, ssm_d, ssm_w_glu, conv_w, norm_final):
    b, s, d = x.shape
    depth = p.shape[0]
    t = b * s
    tm = 512
    assert s % DIL_TILE == 0 and s % tm == 0 and d == 2 * DIFF_WIDTH
    h = x.reshape(t, d)
    for i in range(depth):
        if i % 2 == 0:
            e = i // 2
            lam_init = 0.8 - 0.6 * math.exp(-0.3 * i)
            pa, pb = _norm_matmul(h, norm_mix[i], attn_w_in[e].astype(BF16),
                                  ((0, 3 * DIFF_WIDTH), (3 * DIFF_WIDTH, 3 * DIFF_WIDTH + 3 * SWA_WIDTH)),
                                  (BF16, F32), tm)
            y1 = _diff_attn(pa.reshape(b, s, -1), diff_lq1[e], diff_lk1[e], diff_lq2[e], diff_lk2[e],
                            diff_sub_gain[e], lam_init, tq=512, bk=512, heads=2).reshape(t, -1)
            y2 = _dil_attn(pb.reshape(b, s, -1)).reshape(t, -1)
            wo, w_glu = attn_w_out[e], None
        else:
            o = i // 2
            u, y2 = _ssm_conv_inproj(h, norm_mix[i], rc_w_in[o].astype(BF16), conv_w[o].astype(F32), s, tm)
            ops = _s5_operators(ssm_lambda_re[o], ssm_lambda_im[o], ssm_log_dt[o], ssm_b_re[o], ssm_b_im[o],
                                ssm_c_re[o], ssm_c_im[o], ssm_d[o])
            y1 = _s5_scan(u.reshape(b, s, -1), ops, min(SSM_TILE, s)).reshape(t, -1)
            wo, w_glu = rc_w_out[o], ssm_w_glu[o].astype(BF16)
        h = _post(h, y1, y2, wo.astype(BF16), norm_mlp[i], w_mlp_in[i].astype(BF16), w_mlp_out[i].astype(BF16),
                  norm_ple[i], p.reshape(depth * t, -1), i, w_ple_proj[i].astype(BF16), w_ple_gate[i].astype(BF16),
                  norm_final, i == depth - 1, tm, w_glu)
    return h.reshape(b, s, d)
```

```python
import functools
import math

import jax
import jax.numpy as jnp
from jax import lax
from jax.experimental import pallas as pl
from jax.experimental.pallas import tpu as pltpu

F32 = jnp.float32
BF16 = jnp.bfloat16

EPS = 1e-6
PLE_DIM = 256
DIFF_DH = 64
DIFF_HEADS = 4
DIFF_WIDTH = 512
SWA_DH = 64
SWA_WIDTH = 512
DILATIONS = (16, 4, 1)
N_BACK = 128
DIL_TILE = N_BACK * 16
DIL_GROUP = 2
SSM_GROUP_CH = 16
SSM_STATE = 64
SSM_WIDTH = 512
SSM_GROUPS = 32
SSM_BUNDLES = 4
SSM_CHUNK = 16
SSM_TILE = 1024
CONV_WIDTH = 512
LANES = 128
NEG_BIG = -1e30
VMEM_LIMIT = 60 * 1024 * 1024


def _cparams(sem):
    return pltpu.CompilerParams(dimension_semantics=sem, vmem_limit_bytes=VMEM_LIMIT)


def _resident(shape):
    nd = len(shape)
    return pl.BlockSpec(shape, lambda *_: (0,) * nd, pipeline_mode=pl.Buffered(1))


def _rms(x, g, eps=EPS):
    return x * lax.rsqrt(jnp.mean(x * x, axis=-1, keepdims=True) + eps) * g


def _sigmoid(x):
    return 1.0 / (1.0 + jnp.exp(-x))


def _gelu_tanh(x):
    c = math.sqrt(2.0 / math.pi)
    return x * (0.5 * (1.0 + jnp.tanh(c * (x + 0.044715 * (x * x * x)))))


def _norm_matmul_kernel(h_ref, g_ref, w_ref, *o_refs, splits):
    xn = _rms(h_ref[...], g_ref[...]).astype(BF16)
    for o_ref, (c0, c1) in zip(o_refs, splits):
        o_ref[...] = jnp.dot(xn, w_ref[:, c0:c1], preferred_element_type=F32).astype(o_ref.dtype)


def _norm_matmul(h, g, w, splits, dtypes, tm):
    t, d = h.shape
    n = w.shape[1]
    return pl.pallas_call(
        functools.partial(_norm_matmul_kernel, splits=splits),
        grid=(t // tm,),
        in_specs=[pl.BlockSpec((tm, d), lambda i: (i, 0)), _resident((1, d)), _resident((d, n))],
        out_specs=[pl.BlockSpec((tm, c1 - c0), lambda i: (i, 0)) for (c0, c1) in splits],
        out_shape=[jax.ShapeDtypeStruct((t, c1 - c0), dt) for (c0, c1), dt in zip(splits, dtypes)],
        compiler_params=_cparams(("parallel",)),
        name="norm_matmul",
    )(h, g.reshape(1, d), w)


def _ssm_conv_inproj_kernel(h_ref, hprev_ref, g_ref, w_ref, cw_ref, u_ref, yd_ref, *, tiles_per_seq):
    w = CONV_WIDTH
    g = g_ref[...]
    proj = jnp.dot(_rms(h_ref[...], g).astype(BF16), w_ref[...], preferred_element_type=F32)
    u_ref[...] = proj[:, :SSM_WIDTH]
    gb = proj[:, SSM_WIDTH:SSM_WIDTH + w]
    z = proj[:, SSM_WIDTH + w:SSM_WIDTH + 2 * w] * proj[:, SSM_WIDTH + 2 * w:]
    prev = jnp.dot(_rms(hprev_ref[...], g).astype(BF16), w_ref[:, SSM_WIDTH + w:], preferred_element_type=F32)
    seq_start = (pl.program_id(0) % tiles_per_seq) == 0
    zh = jnp.where(seq_start, 0.0, prev[:, :w] * prev[:, w:])
    row = lax.broadcasted_iota(jnp.int32, z.shape, 0)
    z1 = jnp.where(row == 0, zh[7:8, :], pltpu.roll(z, 1, 0))
    z2 = jnp.where(row == 0, zh[6:7, :], jnp.where(row == 1, zh[7:8, :], pltpu.roll(z, 2, 0)))
    cw = cw_ref[...]
    yd_ref[...] = (gb * (cw[0:1, :] * z + cw[1:2, :] * z1 + cw[2:3, :] * z2)).astype(yd_ref.dtype)


def _ssm_conv_inproj(h, g, w, conv_w, seq, tm):
    t, d = h.shape
    halo = 8
    return pl.pallas_call(
        functools.partial(_ssm_conv_inproj_kernel, tiles_per_seq=seq // tm),
        grid=(t // tm,),
        in_specs=[pl.BlockSpec((tm, d), lambda i: (i, 0)),
                  pl.BlockSpec((halo, d), lambda i: (jnp.maximum(i * (tm // halo) - 1, 0), 0)),
                  _resident((1, d)), _resident(w.shape), _resident(conv_w.shape)],
        out_specs=[pl.BlockSpec((tm, SSM_WIDTH), lambda i: (i, 0)), pl.BlockSpec((tm, CONV_WIDTH), lambda i: (i, 0))],
        out_shape=[jax.ShapeDtypeStruct((t, SSM_WIDTH), F32), jax.ShapeDtypeStruct((t, CONV_WIDTH), BF16)],
        compiler_params=_cparams(("parallel",)),
        name="ssm_conv_inproj",
    )(h, h, g.reshape(1, d), w, conv_w)


def _post_kernel(h_ref, y1_ref, y2_ref, wo_ref, gm_ref, w1_ref, w2_ref, gp_ref, p_ref, wp_ref, wg_ref,
                 gf_ref, *rest, ff_chunk, final, glu):
    o_ref = rest[-1]
    k1 = y1_ref.shape[1]
    h = h_ref[...]
    y1 = y1_ref[...]
    if glu:
        y1 = _gelu_tanh(y1)
        y1 = (y1 * _sigmoid(jnp.dot(y1.astype(BF16), rest[0][...], preferred_element_type=F32))).astype(BF16)
    h = h + jnp.dot(y1, wo_ref[:k1, :], preferred_element_type=F32)
    h = h + jnp.dot(y2_ref[...], wo_ref[k1:, :], preferred_element_type=F32)
    hn = _rms(h, gm_ref[...]).astype(BF16)
    d_ff = w1_ref.shape[1]
    acc = jnp.zeros_like(h)
    for c in range(d_ff // ff_chunk):
        a = jnp.dot(hn, w1_ref[:, c * ff_chunk:(c + 1) * ff_chunk], preferred_element_type=F32)
        a = jnp.square(jnp.maximum(a, 0.0)).astype(BF16)
        acc = acc + jnp.dot(a, w2_ref[c * ff_chunk:(c + 1) * ff_chunk, :], preferred_element_type=F32)
    h = h + acc
    hn = _rms(h, gp_ref[...]).astype(BF16)
    gate = _sigmoid(jnp.dot(hn, wg_ref[...], preferred_element_type=F32))
    emb = jnp.dot(p_ref[...].astype(BF16), wp_ref[...], preferred_element_type=F32)
    h = h + emb * gate
    if final:
        h = _rms(h, gf_ref[...])
    o_ref[...] = h


def _post(h, y1, y2, wo, gm, w1, w2, gp, p, layer, wp, wg, gf, final, tm, w_glu=None):
    t, d = h.shape
    row = lambda width: pl.BlockSpec((tm, width), lambda i: (i, 0))
    p_row = pl.BlockSpec((tm, p.shape[1]), lambda i: (layer * (t // tm) + i, 0))
    glu = w_glu is not None
    return pl.pallas_call(
        functools.partial(_post_kernel, ff_chunk=1024, final=final, glu=glu),
        grid=(t // tm,),
        in_specs=[row(d), row(y1.shape[1]), row(y2.shape[1]), _resident(wo.shape), _resident((1, d)),
                  _resident(w1.shape), _resident(w2.shape), _resident((1, d)), p_row,
                  _resident(wp.shape), _resident(wg.shape), _resident((1, d))]
                 + ([_resident(w_glu.shape)] if glu else []),
        out_specs=row(d),
        out_shape=jax.ShapeDtypeStruct((t, d), F32),
        compiler_params=_cparams(("parallel",)),
        name="post_mixer",
    )(h, y1, y2, wo, gm.reshape(1, d), w1, w2, gp.reshape(1, d), p, wp, wg, gf.reshape(1, d),
      *([w_glu] if glu else []))


def _stack_heads(x, scale):
    lane = lax.broadcasted_iota(jnp.int32, x.shape, 1)
    zero = jnp.zeros_like(x)
    lo = jnp.where(lane < 64, x, zero)
    hi = jnp.where(lane >= 64, x, zero)
    return jnp.concatenate([lo, hi], axis=0) * scale


def _diff_attn_kernel(q_ref, k_ref, v_ref, lq1_ref, lk1_ref, lq2_ref, lk2_ref, sg_ref, bias_ref, o_ref,
                      m_s, l_s, acc_s, s_s, *, tq, bk, lam_init):
    qi = pl.program_id(2)
    heads = m_s.shape[0]
    lanes = lambda h: pl.ds(h * LANES, LANES)
    scale = jnp.asarray(DIFF_DH ** -0.5, BF16)
    qs = [_stack_heads(q_ref[:, lanes(h)], scale) for h in range(heads)]
    n_full = (qi * tq) // bk

    def scores(j, slot):
        keys = pl.ds(pl.multiple_of(j * bk, bk), bk)
        for h in range(heads):
            s_s[slot, h] = lax.dot_general(qs[h], k_ref[keys, lanes(h)], (((1,), (1,)), ((), ())),
                                           preferred_element_type=F32)

    def update(j, slot, masked):
        keys = pl.ds(pl.multiple_of(j * bk, bk), bk)
        for h in range(heads):
            s = s_s[slot, h]
            if masked:
                s = s + bias_ref[(qi * tq - j * bk) // tq]
            m = m_s[h]
            m_new = jnp.maximum(m, jnp.max(s, axis=-1, keepdims=True))
            alpha = jnp.exp(m - m_new)
            p = jnp.exp(s - jnp.concatenate([m_new] * (bk // LANES), axis=1))
            m_s[h] = m_new
            l_s[h] = alpha * l_s[h] + sum(p[:, c * LANES:(c + 1) * LANES] for c in range(bk // LANES))
            acc_s[h] = alpha * acc_s[h] + jnp.dot(p.astype(BF16), v_ref[keys, lanes(h)],
                                                  preferred_element_type=F32)

    m_s[...] = jnp.full(m_s.shape, NEG_BIG, F32)
    l_s[...] = jnp.zeros(l_s.shape, F32)
    acc_s[...] = jnp.zeros(acc_s.shape, F32)

    scores(0, 0)

    def pair(i, carry):
        j = 2 * i
        scores(j + 1, 1)
        update(j, 0, False)
        scores(j + 2, 0)
        update(j + 1, 1, False)
        return carry

    lax.fori_loop(0, n_full // 2, pair, 0)
    odd = n_full % 2 == 1

    @pl.when(odd)
    def _():
        scores(n_full, 1)
        update(n_full - 1, 0, False)
        update(n_full, 1, True)

    @pl.when(jnp.logical_not(odd))
    def _():
        update(n_full, 0, True)
    lam = (jnp.exp(jnp.sum(lq1_ref[...] * lk1_ref[...], keepdims=True))
           - jnp.exp(jnp.sum(lq2_ref[...] * lk2_ref[...], keepdims=True)) + lam_init)
    for h in range(heads):
        o = acc_s[h] / jnp.sum(l_s[h], axis=-1, keepdims=True)
        o = o[:tq] - lam * o[tq:]
        o = o * lax.rsqrt(jnp.mean(o * o, axis=-1, keepdims=True) + 1e-5) * sg_ref[...]
        o_ref[:, lanes(h)] = (o * (1.0 - lam_init)).astype(o_ref.dtype)


def _diff_attn(proj, lq1, lk1, lq2, lk2, sub_gain, lam_init, tq, bk, heads):
    b, s, _ = proj.shape
    groups, width = DIFF_HEADS // heads, heads * LANES
    vec = lambda a: a.astype(F32).reshape(1, -1)
    off = jnp.arange(bk // tq)[:, None, None] * tq
    qpos = off + (jnp.arange(2 * tq) % tq)[None, :, None]
    bias = jnp.where(jnp.arange(bk)[None, None, :] <= qpos, 0.0, NEG_BIG).astype(F32)
    return pl.pallas_call(
        functools.partial(_diff_attn_kernel, tq=tq, bk=bk, lam_init=lam_init),
        grid=(b, groups, s // tq),
        in_specs=[pl.BlockSpec((None, tq, width), lambda bi, h, qi: (bi, qi, h)),
                  pl.BlockSpec((None, s, width), lambda bi, h, qi: (bi, 0, groups + h)),
                  pl.BlockSpec((None, s, width), lambda bi, h, qi: (bi, 0, 2 * groups + h)),
                  _resident((1, DIFF_DH)), _resident((1, DIFF_DH)), _resident((1, DIFF_DH)),
                  _resident((1, DIFF_DH)), _resident((1, 2 * DIFF_DH)), _resident(bias.shape)],
        out_specs=pl.BlockSpec((None, tq, width), lambda bi, h, qi: (bi, qi, h)),
        out_shape=jax.ShapeDtypeStruct((b, s, DIFF_WIDTH), BF16),
        scratch_shapes=[pltpu.VMEM((heads, 2 * tq, LANES), F32)] * 3 + [pltpu.VMEM((2, heads, 2 * tq, bk), F32)],
        compiler_params=_cparams(("parallel", "parallel", "arbitrary")),
        name="diff_attn",
    )(proj, proj, proj, vec(lq1), vec(lk1), vec(lq2), vec(lk2), vec(sub_gain), bias)


def _dil_attn_kernel(q_ref, kp_ref, kc_ref, vp_ref, vc_ref, o_ref, k_s, v_s, acc_s, m_s, l_s, bias_s, s_s,
                     q4_s, k4_s, v4_s, acc4_s, m4_s, l4_s):
    first = pl.program_id(2) == 0
    tile = q_ref.shape[0]
    k_s[pl.ds(0, tile), :] = kp_ref[...]
    k_s[pl.ds(tile, tile), :] = kc_ref[...]
    v_s[pl.ds(0, tile), :] = vp_ref[...]
    v_s[pl.ds(tile, tile), :] = vc_ref[...]
    qt, kt = tile // 4, tile // 2
    for rho in range(4):
        q4_s[pl.ds(rho * qt, qt), :] = q_ref[pl.ds(rho, qt, stride=4), :]
        k4_s[pl.ds(rho * kt, qt), :] = kp_ref[pl.ds(rho, qt, stride=4), :]
        k4_s[pl.ds(rho * kt + qt, qt), :] = kc_ref[pl.ds(rho, qt, stride=4), :]
        v4_s[pl.ds(rho * kt, qt), :] = vp_ref[pl.ds(rho, qt, stride=4), :]
        v4_s[pl.ds(rho * kt + qt, qt), :] = vc_ref[pl.ds(rho, qt, stride=4), :]
    scale = jnp.asarray(SWA_DH ** -0.5, BF16)
    lane = lax.broadcasted_iota(jnp.int32, (N_BACK, LANES), 1)
    lo = lane < 64
    r = lax.broadcasted_iota(jnp.int32, (2 * N_BACK, 2 * N_BACK), 0)
    qidx = jnp.where(r >= N_BACK, r - N_BACK, r)
    kk = lax.broadcasted_iota(jnp.int32, (2 * N_BACK, 2 * N_BACK), 1)
    band = (kk >= qidx) & (kk <= qidx + N_BACK)
    bias_s[0] = jnp.where(band, 0.0, NEG_BIG)
    bias_s[1] = jnp.where(band & jnp.logical_or(jnp.logical_not(first), kk >= N_BACK), 0.0, NEG_BIG)

    def windows(dil, n):
        rho, rest = n % 4, n // 4
        if dil == 16:
            rows = pl.ds(rho * qt + rest, N_BACK, stride=4)
            return (q4_s, rows), (k4_s, v4_s, pl.ds(rho * kt + rest, 2 * N_BACK, stride=4)), None, (state4, rows)
        if dil == 4:
            rows = pl.ds(rho * qt + rest * N_BACK, N_BACK)
            keys = pl.ds(rho * kt + qt - N_BACK + rest * N_BACK, 2 * N_BACK)
            return ((q4_s, rows), (k4_s, v4_s, keys), (state4, rows),
                    (state, pl.ds(rho + rest * 4 * N_BACK, N_BACK, stride=4)))
        rows = pl.ds(n * N_BACK, N_BACK)
        return (q_ref, rows), (k_s, v_s, pl.ds(tile + (n - 1) * N_BACK, 2 * N_BACK)), (state, rows), None

    state, state4 = (acc_s, m_s, l_s), (acc4_s, m4_s, l4_s)

    def score(dil, n, slot):
        (q_buf, rows), (k_buf, _, krows), _, _ = windows(dil, n)
        qs = _stack_heads(q_buf[rows, :].astype(BF16), scale)
        kb = k_buf[krows, :].astype(BF16)
        s_s[slot] = lax.dot_general(qs, kb, (((1,), (1,)), ((), ())), preferred_element_type=F32)

    def finish(dil, n, slot):
        (_, rows), (_, v_buf, krows), old, new = windows(dil, n)
        vb = v_buf[krows, :].astype(BF16)
        s = s_s[slot] + bias_s[1 if n < dil else 0]
        m2 = jnp.max(s, axis=-1, keepdims=True)
        p = jnp.exp(s - m2)
        l2 = jnp.sum(p, axis=-1, keepdims=True)
        pv = jnp.dot(p.astype(BF16), vb, preferred_element_type=F32)
        acc_c = jnp.where(lo, pv[:N_BACK], pv[N_BACK:])
        m_c = jnp.where(lo, m2[:N_BACK], m2[N_BACK:])
        l_c = jnp.where(lo, l2[:N_BACK], l2[N_BACK:])
        if old is not None:
            (acc_o, m_o, l_o), orows = old
            m_o = m_o[orows, :]
            m_n = jnp.maximum(m_o, m_c)
            a_o = jnp.exp(m_o - m_n)
            a_c = jnp.exp(m_c - m_n)
            acc_c = a_o * acc_o[orows, :] + a_c * acc_c
            l_c = a_o * l_o[orows, :] + a_c * l_c
            m_c = m_n
        if new is None:
            o_ref[rows, :] = (acc_c / l_c).astype(o_ref.dtype)
        else:
            (acc_n, m_n, l_n), nrows = new
            acc_n[nrows, :] = acc_c
            m_n[nrows, :] = m_c
            l_n[nrows, :] = l_c

    groups = [(dil, g) for dil in DILATIONS for g in range(DIL_TILE // N_BACK // DIL_GROUP)]

    def score_group(i):
        dil, g = groups[i]
        for e in range(DIL_GROUP):
            score(dil, g * DIL_GROUP + e, (i % 2) * DIL_GROUP + e)

    score_group(0)
    for i, (dil, g) in enumerate(groups):
        if i + 1 < len(groups):
            score_group(i + 1)
        for e in range(DIL_GROUP):
            finish(dil, g * DIL_GROUP + e, (i % 2) * DIL_GROUP + e)


def _dil_attn(proj):
    b, s, _ = proj.shape
    pairs = SWA_WIDTH // LANES
    cur = lambda part: pl.BlockSpec((None, DIL_TILE, LANES), lambda bi, hp, i: (bi, i, part * pairs + hp))
    prev = lambda part: pl.BlockSpec((None, DIL_TILE, LANES),
                                     lambda bi, hp, i: (bi, jnp.maximum(i - 1, 0), part * pairs + hp))
    return pl.pallas_call(
        _dil_attn_kernel,
        grid=(b, pairs, s // DIL_TILE),
        in_specs=[cur(0), prev(1), cur(1), prev(2), cur(2)],
        out_specs=pl.BlockSpec((None, DIL_TILE, LANES), lambda bi, hp, i: (bi, i, hp)),
        out_shape=jax.ShapeDtypeStruct((b, s, SWA_WIDTH), BF16),
        scratch_shapes=[pltpu.VMEM((2 * DIL_TILE, LANES), F32), pltpu.VMEM((2 * DIL_TILE, LANES), F32),
                        pltpu.VMEM((DIL_TILE, LANES), F32), pltpu.VMEM((DIL_TILE, LANES), F32),
                        pltpu.VMEM((DIL_TILE, LANES), F32), pltpu.VMEM((2, 2 * N_BACK, 2 * N_BACK), F32),
                        pltpu.VMEM((2 * DIL_GROUP, 2 * N_BACK, 2 * N_BACK), F32),
                        pltpu.VMEM((DIL_TILE, LANES), F32), pltpu.VMEM((2 * DIL_TILE, LANES), F32),
                        pltpu.VMEM((2 * DIL_TILE, LANES), F32), pltpu.VMEM((DIL_TILE, LANES), F32),
                        pltpu.VMEM((DIL_TILE, LANES), F32), pltpu.VMEM((DIL_TILE, LANES), F32)],
        compiler_params=_cparams(("parallel", "parallel", "arbitrary")),
        name="dilated_attn",
    )(proj, proj, proj, proj, proj)


def _bundle_block_diag(x):
    nb, gb = SSM_BUNDLES, SSM_GROUPS // SSM_BUNDLES
    _, c, p = x.shape
    x = x.reshape(nb, gb, c, p)
    out = jnp.zeros((nb, gb, c, gb, p), x.dtype)
    for g in range(gb):
        out = out.at[:, g, :, g, :].set(x[:, g])
    return out.reshape(nb, gb * c, gb * p)


def _s5_prep_kernel(bb_re_ref, bb_im_ref, c_re_ref, c_im_ref, pw_re_ref, pw_im_ref, toep_ref, ts_ref, fs_ref):
    lc = SSM_CHUNK
    hp = lax.Precision.HIGHEST
    nt = (((1,), (1,)), ((), ()))
    bb_re, bb_im, c_re, c_im = bb_re_ref[...], bb_im_ref[...], c_re_ref[...], c_im_ref[...]
    half = bb_re.shape[1]
    blk = lambda i: pl.ds(i * LANES, LANES)
    zero = jnp.zeros((LANES, LANES), toep_ref.dtype)
    for j in range(lc):
        a_re, a_im = pw_re_ref[j:j + 1, :], pw_im_ref[j:j + 1, :]
        ab_re = bb_re * a_re - bb_im * a_im
        ab_im = bb_re * a_im + bb_im * a_re
        lag = (lax.dot_general(ab_re, c_re, nt, precision=hp, preferred_element_type=F32)
               - lax.dot_general(ab_im, c_im, nt, precision=hp, preferred_element_type=F32))
        lag = lag.astype(toep_ref.dtype)
        for s in range(lc - j):
            toep_ref[blk(s), blk(s + j)] = lag
        for t in range(j):
            toep_ref[blk(j), blk(t)] = zero
        s = lc - 1 - j
        ts_ref[blk(s), pl.ds(0, half)] = ab_re.astype(ts_ref.dtype)
        ts_ref[blk(s), pl.ds(half, half)] = ab_im.astype(ts_ref.dtype)
    for t in range(lc):
        a_re, a_im = pw_re_ref[t + 1:t + 2, :], pw_im_ref[t + 1:t + 2, :]
        ca_re = c_re * a_re - c_im * a_im
        ca_im = c_re * a_im + c_im * a_re
        fs_ref[pl.ds(0, half), blk(t)] = ca_re.T.astype(fs_ref.dtype)
        fs_ref[pl.ds(half, half), blk(t)] = (-ca_im).T.astype(fs_ref.dtype)


def _s5_operators(lam_re, lam_im, log_dt, b_re, b_im, c_re, c_im, d_skip):
    lc = SSM_CHUNK
    nb, gb, st = SSM_BUNDLES, SSM_GROUPS // SSM_BUNDLES, SSM_STATE
    lr, li = lam_re.astype(F32), lam_im.astype(F32)
    dt = jnp.exp(log_dt.astype(F32))[:, None]
    mag = jnp.exp(lr * dt)
    abar_re, abar_im = mag * jnp.cos(li * dt), mag * jnp.sin(li * dt)
    den = lr * lr + li * li
    nr, ni = abar_re - 1.0, abar_im
    coef_re = (nr * lr + ni * li) / den
    coef_im = (ni * lr - nr * li) / den
    br, bi = b_re.astype(F32), b_im.astype(F32)
    bb_re = coef_re[..., None] * br - coef_im[..., None] * bi
    bb_im = coef_re[..., None] * bi + coef_im[..., None] * br
    j = jnp.arange(lc + 1, dtype=F32)[:, None, None]
    pmag = jnp.exp(j * (lr * dt)[None])
    pw_re, pw_im = pmag * jnp.cos(j * (li * dt)[None]), pmag * jnp.sin(j * (li * dt)[None])
    per_bundle = lambda pw: jnp.transpose(pw.reshape(lc + 1, nb, gb * st), (1, 0, 2))
    pw_re, pw_im = per_bundle(pw_re), per_bundle(pw_im)
    small = [_bundle_block_diag(jnp.swapaxes(bb_re, 1, 2)), _bundle_block_diag(jnp.swapaxes(bb_im, 1, 2)),
             _bundle_block_diag(c_re.astype(F32)), _bundle_block_diag(c_im.astype(F32)), pw_re, pw_im]
    width, states = lc * LANES, 2 * gb * st
    bundle = lambda shape: pl.BlockSpec((None,) + shape, lambda bi: (bi, 0, 0))
    toep, to_state, from_state = pl.pallas_call(
        _s5_prep_kernel,
        grid=(nb,),
        in_specs=[bundle(a.shape[1:]) for a in small],
        out_specs=[bundle((width, width)), bundle((width, states)), bundle((states, width))],
        out_shape=[jax.ShapeDtypeStruct((nb, width, width), BF16), jax.ShapeDtypeStruct((nb, width, states), BF16),
                   jax.ShapeDtypeStruct((nb, states, width), BF16)],
        compiler_params=_cparams(("parallel",)),
        name="s5_prep",
    )(*small)
    a_lc_re, a_lc_im = pw_re[:, lc], pw_im[:, lc]
    dec_a = jnp.concatenate([a_lc_re, a_lc_re], axis=-1)[:, None, :]
    dec_b = jnp.concatenate([-a_lc_im, a_lc_im], axis=-1)[:, None, :]
    skip = d_skip.astype(F32).reshape(nb, 1, LANES)
    return toep, to_state, from_state, dec_a, dec_b, skip


def _s5_kernel(u_ref, toep_ref, ts_ref, fs_ref, da_ref, db_ref, skip_ref, y_ref, lhs_s, ug_s, loc_s, cin_s,
               state_s):
    lc = SSM_CHUNK
    batch, tile, _ = u_ref.shape
    chunks = tile // lc
    slabs = state_s.shape[0]

    for b in range(batch):
        for s in range(lc):
            piece = u_ref[b, pl.ds(s, chunks, stride=lc), :]
            ug_s[s, pl.ds(b * chunks, chunks), :] = piece
            lhs_s[pl.ds(b * chunks, chunks), pl.ds(s * LANES, LANES)] = piece.astype(BF16)
    quarter = lc * LANES // 4
    ys = [jnp.dot(lhs_s[:, pl.ds(0, (i + 1) * quarter)], toep_ref[pl.ds(0, (i + 1) * quarter), pl.ds(i * quarter, quarter)],
                  preferred_element_type=F32) for i in range(4)]
    loc = jnp.dot(lhs_s[...], ts_ref[...], preferred_element_type=F32)
    for k in range(slabs):
        for b in range(batch):
            loc_s[k, pl.ds(b, chunks, stride=batch), :] = loc[b * chunks:(b + 1) * chunks, k * LANES:(k + 1) * LANES]

    @pl.when(pl.program_id(1) == 0)
    def _():
        state_s[...] = jnp.zeros(state_s.shape, F32)

    da = [da_ref[:, k * LANES:(k + 1) * LANES] for k in range(slabs)]
    db = [db_ref[:, k * LANES:(k + 1) * LANES] for k in range(slabs)]

    def step(n, x):
        rows = pl.ds(pl.multiple_of(n * batch, batch), batch)
        for k in range(slabs):
            cin_s[k, rows, :] = x[k]
        return tuple(da[k] * x[k] + db[k] * x[(k + slabs // 2) % slabs] + loc_s[k, rows, :] for k in range(slabs))

    x = lax.fori_loop(0, chunks, step, tuple(state_s[k] for k in range(slabs)))
    for k in range(slabs):
        state_s[k] = x[k]
    cin = jnp.concatenate(
        [jnp.concatenate([cin_s[k, pl.ds(b, chunks, stride=batch), :] for b in range(batch)], axis=0)
         for k in range(slabs)], axis=1).astype(BF16)
    ys = [ys[i] + jnp.dot(cin, fs_ref[:, pl.ds(i * quarter, quarter)], preferred_element_type=F32) for i in range(4)]
    skip = skip_ref[...]
    per_quarter = lc // 4
    for b in range(batch):
        for t in range(lc):
            rows = pl.ds(b * chunks, chunks)
            cols = (t % per_quarter) * LANES
            y_ref[b, pl.ds(t, chunks, stride=lc), :] = (ys[t // per_quarter][b * chunks:(b + 1) * chunks, cols:cols + LANES]
                                                        + ug_s[t, rows, :] * skip)


def _s5_scan(u, ops, tile):
    toep, to_state, from_state, dec_a, dec_b, skip = ops
    batch, seq, _ = u.shape
    lc = SSM_CHUNK
    rows = batch * (tile // lc)
    slabs = to_state.shape[2] // LANES
    act = pl.BlockSpec((batch, tile, LANES), lambda bi, si: (0, si, bi))
    per_bundle = lambda a, **kw: pl.BlockSpec((None,) + a.shape[1:], lambda bi, si: (bi, 0, 0), **kw)
    once = dict(pipeline_mode=pl.Buffered(1))
    return pl.pallas_call(
        _s5_kernel,
        grid=(SSM_BUNDLES, seq // tile),
        in_specs=[act, per_bundle(toep, **once), per_bundle(to_state, **once), per_bundle(from_state, **once),
                  per_bundle(dec_a), per_bundle(dec_b), per_bundle(skip)],
        out_specs=act,
        out_shape=jax.ShapeDtypeStruct(u.shape, F32),
        scratch_shapes=[pltpu.VMEM((rows, lc * LANES), BF16), pltpu.VMEM((lc, rows, LANES), F32),
                        pltpu.VMEM((slabs, rows, LANES), F32), pltpu.VMEM((slabs, rows, LANES), F32),
                        pltpu.VMEM((slabs, batch, LANES), F32)],
        compiler_params=_cparams(("parallel", "arbitrary")),
        name="s5_scan",
    )(u, toep, to_state, from_state, dec_a, dec_b, skip)


def kernel(x, p, norm_mix, norm_mlp, norm_ple, w_mlp_in, w_mlp_out, w_ple_proj, w_ple_gate, attn_w_in, attn_w_out, diff_lq1, diff_lk1, diff_lq2, diff_lk2, diff_sub_gain, rc_w_in, rc_w_out, ssm_lambda_re, ssm_lambda_im, ssm_log_dt, ssm_b_re, ssm_b_im, ssm_c_re, ssm_c_im, ssm_d, ssm_w_glu, conv_w, norm_final):
    b, s, d = x.shape
    depth = p.shape[0]
    t = b * s
    tm = 512
    assert s % DIL_TILE == 0 and s % tm == 0 and d == 2 * DIFF_WIDTH
    h = x.reshape(t, d)
    for i in range(depth):
        if i % 2 == 0:
            e = i // 2
            lam_init = 0.8 - 0.6 * math.exp(-0.3 * i)
            pa, pb = _norm_matmul(h, norm_mix[i], attn_w_in[e].astype(BF16),
                                  ((0, 3 * DIFF_WIDTH), (3 * DIFF_WIDTH, 3 * DIFF_WIDTH + 3 * SWA_WIDTH)),
                                  (BF16, F32), tm)
            y1 = _diff_attn(pa.reshape(b, s, -1), diff_lq1[e], diff_lk1[e], diff_lq2[e], diff_lk2[e],
                            diff_sub_gain[e], lam_init, tq=512, bk=512, heads=2).reshape(t, -1)
            y2 = _dil_attn(pb.reshape(b, s, -1)).reshape(t, -1)
            wo, w_glu = attn_w_out[e], None
        else:
            o = i // 2
            u, y2 = _ssm_conv_inproj(h, norm_mix[i], rc_w_in[o].astype(BF16), conv_w[o].astype(F32), s, tm)
            ops = _s5_operators(ssm_lambda_re[o], ssm_lambda_im[o], ssm_log_dt[o], ssm_b_re[o], ssm_b_im[o],
                                ssm_c_re[o], ssm_c_im[o], ssm_d[o])
            y1 = _s5_scan(u.reshape(b, s, -1), ops, min(SSM_TILE, s)).reshape(t, -1)
            wo, w_glu = rc_w_out[o], ssm_w_glu[o].astype(BF16)
        h = _post(h, y1, y2, wo.astype(BF16), norm_mlp[i], w_mlp_in[i].astype(BF16), w_mlp_out[i].astype(BF16),
                  norm_ple[i], p.reshape(depth * t, -1), i, w_ple_proj[i].astype(BF16), w_ple_gate[i].astype(BF16),
                  norm_final, i == depth - 1, tm, w_glu)
    return h.reshape(b, s, d)
```

```python
import functools
import math

import jax
import jax.numpy as jnp
from jax import lax
from jax.experimental import pallas as pl
from jax.experimental.pallas import tpu as pltpu

F32 = jnp.float32
BF16 = jnp.bfloat16

EPS = 1e-6
PLE_DIM = 256
DIFF_DH = 64
DIFF_HEADS = 4
DIFF_WIDTH = 512
SWA_DH = 64
SWA_WIDTH = 512
DILATIONS = (16, 4, 1)
N_BACK = 128
DIL_TILE = N_BACK * 16
DIL_GROUP = 2
SSM_GROUP_CH = 16
SSM_STATE = 64
SSM_WIDTH = 512
SSM_GROUPS = 32
SSM_BUNDLES = 4
SSM_CHUNK = 16
SSM_TILE = 1024
CONV_WIDTH = 512
LANES = 128
NEG_BIG = -1e30
VMEM_LIMIT = 60 * 1024 * 1024


def _cparams(sem):
    return pltpu.CompilerParams(dimension_semantics=sem, vmem_limit_bytes=VMEM_LIMIT)


def _resident(shape):
    nd = len(shape)
    return pl.BlockSpec(shape, lambda *_: (0,) * nd, pipeline_mode=pl.Buffered(1))


def _rms(x, g, eps=EPS):
    return x * lax.rsqrt(jnp.mean(x * x, axis=-1, keepdims=True) + eps) * g


def _sigmoid(x):
    return 1.0 / (1.0 + jnp.exp(-x))


def _gelu_tanh(x):
    c = math.sqrt(2.0 / math.pi)
    return x * (0.5 * (1.0 + jnp.tanh(c * (x + 0.044715 * (x * x * x)))))


def _norm_matmul_kernel(h_ref, g_ref, w_ref, *o_refs, splits):
    xn = _rms(h_ref[...], g_ref[...]).astype(BF16)
    for o_ref, (c0, c1) in zip(o_refs, splits):
        o_ref[...] = jnp.dot(xn, w_ref[:, c0:c1], preferred_element_type=F32).astype(o_ref.dtype)


def _norm_matmul(h, g, w, splits, dtypes, tm):
    t, d = h.shape
    n = w.shape[1]
    return pl.pallas_call(
        functools.partial(_norm_matmul_kernel, splits=splits),
        grid=(t // tm,),
        in_specs=[pl.BlockSpec((tm, d), lambda i: (i, 0)), _resident((1, d)), _resident((d, n))],
        out_specs=[pl.BlockSpec((tm, c1 - c0), lambda i: (i, 0)) for (c0, c1) in splits],
        out_shape=[jax.ShapeDtypeStruct((t, c1 - c0), dt) for (c0, c1), dt in zip(splits, dtypes)],
        compiler_params=_cparams(("parallel",)),
        name="norm_matmul",
    )(h, g.reshape(1, d), w)


def _ssm_conv_inproj_kernel(h_ref, hprev_ref, g_ref, w_ref, cw_ref, u_ref, yd_ref, *, tiles_per_seq):
    w = CONV_WIDTH
    g = g_ref[...]
    xn = _rms(h_ref[...], g).astype(BF16)
    z = (jnp.dot(xn, w_ref[:, SSM_WIDTH + w:SSM_WIDTH + 2 * w], preferred_element_type=F32)
         * jnp.dot(xn, w_ref[:, SSM_WIDTH + 2 * w:], preferred_element_type=F32))
    gb = jnp.dot(xn, w_ref[:, SSM_WIDTH:SSM_WIDTH + w], preferred_element_type=F32)
    u_ref[...] = jnp.dot(xn, w_ref[:, :SSM_WIDTH], preferred_element_type=F32)
    prev = jnp.dot(_rms(hprev_ref[...], g).astype(BF16), w_ref[:, SSM_WIDTH + w:], preferred_element_type=F32)
    seq_start = (pl.program_id(0) % tiles_per_seq) == 0
    zh = jnp.where(seq_start, 0.0, prev[:, :w] * prev[:, w:])
    row = lax.broadcasted_iota(jnp.int32, z.shape, 0)
    z1 = jnp.where(row == 0, zh[7:8, :], pltpu.roll(z, 1, 0))
    z2 = jnp.where(row == 0, zh[6:7, :], jnp.where(row == 1, zh[7:8, :], pltpu.roll(z, 2, 0)))
    cw = cw_ref[...]
    yd_ref[...] = (gb * (cw[0:1, :] * z + cw[1:2, :] * z1 + cw[2:3, :] * z2)).astype(yd_ref.dtype)


def _ssm_conv_inproj(h, g, w, conv_w, seq, tm):
    t, d = h.shape
    halo = 8
    return pl.pallas_call(
        functools.partial(_ssm_conv_inproj_kernel, tiles_per_seq=seq // tm),
        grid=(t // tm,),
        in_specs=[pl.BlockSpec((tm, d), lambda i: (i, 0)),
                  pl.BlockSpec((halo, d), lambda i: (jnp.maximum(i * (tm // halo) - 1, 0), 0)),
                  _resident((1, d)), _resident(w.shape), _resident(conv_w.shape)],
        out_specs=[pl.BlockSpec((tm, SSM_WIDTH), lambda i: (i, 0)), pl.BlockSpec((tm, CONV_WIDTH), lambda i: (i, 0))],
        out_shape=[jax.ShapeDtypeStruct((t, SSM_WIDTH), F32), jax.ShapeDtypeStruct((t, CONV_WIDTH), BF16)],
        compiler_params=_cparams(("parallel",)),
        name="ssm_conv_inproj",
    )(h, h, g.reshape(1, d), w, conv_w)


def _post_kernel(h_ref, y1_ref, y2_ref, wo_ref, gm_ref, w1_ref, w2_ref, gp_ref, p_ref, wp_ref, wg_ref,
                 gf_ref, *rest, ff_chunk, final, glu):
    o_ref = rest[-1]
    k1 = y1_ref.shape[1]
    h = h_ref[...]
    y1 = y1_ref[...]
    if glu:
        y1 = _gelu_tanh(y1)
        y1 = (y1 * _sigmoid(jnp.dot(y1.astype(BF16), rest[0][...], preferred_element_type=F32))).astype(BF16)
    h = h + jnp.dot(y1, wo_ref[:k1, :], preferred_element_type=F32)
    h = h + jnp.dot(y2_ref[...], wo_ref[k1:, :], preferred_element_type=F32)
    hn = _rms(h, gm_ref[...]).astype(BF16)
    d_ff = w1_ref.shape[1]
    acc = jnp.zeros_like(h)
    for c in range(d_ff // ff_chunk):
        a = jnp.dot(hn, w1_ref[:, c * ff_chunk:(c + 1) * ff_chunk], preferred_element_type=F32)
        a = jnp.square(jnp.maximum(a, 0.0)).astype(BF16)
        acc = acc + jnp.dot(a, w2_ref[c * ff_chunk:(c + 1) * ff_chunk, :], preferred_element_type=F32)
    h = h + acc
    hn = _rms(h, gp_ref[...]).astype(BF16)
    gate = _sigmoid(jnp.dot(hn, wg_ref[...], preferred_element_type=F32))
    emb = jnp.dot(p_ref[...].astype(BF16), wp_ref[...], preferred_element_type=F32)
    h = h + emb * gate
    if final:
        h = _rms(h, gf_ref[...])
    o_ref[...] = h


def _post(h, y1, y2, wo, gm, w1, w2, gp, p, layer, wp, wg, gf, final, tm, w_glu=None):
    t, d = h.shape
    row = lambda width: pl.BlockSpec((tm, width), lambda i: (i, 0))
    p_row = pl.BlockSpec((tm, p.shape[1]), lambda i: (layer * (t // tm) + i, 0))
    glu = w_glu is not None
    return pl.pallas_call(
        functools.partial(_post_kernel, ff_chunk=1024, final=final, glu=glu),
        grid=(t // tm,),
        in_specs=[row(d), row(y1.shape[1]), row(y2.shape[1]), _resident(wo.shape), _resident((1, d)),
                  _resident(w1.shape), _resident(w2.shape), _resident((1, d)), p_row,
                  _resident(wp.shape), _resident(wg.shape), _resident((1, d))]
                 + ([_resident(w_glu.shape)] if glu else []),
        out_specs=row(d),
        out_shape=jax.ShapeDtypeStruct((t, d), F32),
        compiler_params=_cparams(("parallel",)),
        name="post_mixer",
    )(h, y1, y2, wo, gm.reshape(1, d), w1, w2, gp.reshape(1, d), p, wp, wg, gf.reshape(1, d),
      *([w_glu] if glu else []))


def _stack_heads(x, scale):
    lane = lax.broadcasted_iota(jnp.int32, x.shape, 1)
    zero = jnp.zeros_like(x)
    lo = jnp.where(lane < 64, x, zero)
    hi = jnp.where(lane >= 64, x, zero)
    return jnp.concatenate([lo, hi], axis=0) * scale


def _diff_attn_kernel(q_ref, k_ref, v_ref, lq1_ref, lk1_ref, lq2_ref, lk2_ref, sg_ref, bias_ref, o_ref,
                      m_s, l_s, acc_s, s_s, *, tq, bk, lam_init):
    qi = pl.program_id(2)
    heads = m_s.shape[0]
    lanes = lambda h: pl.ds(h * LANES, LANES)
    scale = jnp.asarray(DIFF_DH ** -0.5, BF16)
    qs = [_stack_heads(q_ref[:, lanes(h)], scale) for h in range(heads)]
    n_full = (qi * tq) // bk

    def scores(j, slot):
        keys = pl.ds(pl.multiple_of(j * bk, bk), bk)
        for h in range(heads):
            s_s[slot, h] = lax.dot_general(qs[h], k_ref[keys, lanes(h)], (((1,), (1,)), ((), ())),
                                           preferred_element_type=F32)

    def update(j, slot, masked):
        keys = pl.ds(pl.multiple_of(j * bk, bk), bk)
        for h in range(heads):
            s = s_s[slot, h]
            if masked:
                s = s + bias_ref[(qi * tq - j * bk) // tq]
            m = m_s[h]
            m_new = jnp.maximum(m, jnp.max(s, axis=-1, keepdims=True))
            alpha = jnp.exp(m - m_new)
            p = jnp.exp(s - jnp.concatenate([m_new] * (bk // LANES), axis=1))
            m_s[h] = m_new
            l_s[h] = alpha * l_s[h] + sum(p[:, c * LANES:(c + 1) * LANES] for c in range(bk // LANES))
            acc_s[h] = alpha * acc_s[h] + jnp.dot(p.astype(BF16), v_ref[keys, lanes(h)],
                                                  preferred_element_type=F32)

    m_s[...] = jnp.full(m_s.shape, NEG_BIG, F32)
    l_s[...] = jnp.zeros(l_s.shape, F32)
    acc_s[...] = jnp.zeros(acc_s.shape, F32)

    scores(0, 0)

    def pair(i, carry):
        j = 2 * i
        scores(j + 1, 1)
        update(j, 0, False)
        scores(j + 2, 0)
        update(j + 1, 1, False)
        return carry

    lax.fori_loop(0, n_full // 2, pair, 0)
    odd = n_full % 2 == 1

    @pl.when(odd)
    def _():
        scores(n_full, 1)
        update(n_full - 1, 0, False)
        update(n_full, 1, True)

    @pl.when(jnp.logical_not(odd))
    def _():
        update(n_full, 0, True)
    lam = (jnp.exp(jnp.sum(lq1_ref[...] * lk1_ref[...], keepdims=True))
           - jnp.exp(jnp.sum(lq2_ref[...] * lk2_ref[...], keepdims=True)) + lam_init)
    for h in range(heads):
        o = acc_s[h] / jnp.sum(l_s[h], axis=-1, keepdims=True)
        o = o[:tq] - lam * o[tq:]
        o = o * lax.rsqrt(jnp.mean(o * o, axis=-1, keepdims=True) + 1e-5) * sg_ref[...]
        o_ref[:, lanes(h)] = (o * (1.0 - lam_init)).astype(o_ref.dtype)


def _diff_attn(proj, lq1, lk1, lq2, lk2, sub_gain, lam_init, tq, bk, heads):
    b, s, _ = proj.shape
    groups, width = DIFF_HEADS // heads, heads * LANES
    vec = lambda a: a.astype(F32).reshape(1, -1)
    off = jnp.arange(bk // tq)[:, None, None] * tq
    qpos = off + (jnp.arange(2 * tq) % tq)[None, :, None]
    bias = jnp.where(jnp.arange(bk)[None, None, :] <= qpos, 0.0, NEG_BIG).astype(F32)
    return pl.pallas_call(
        functools.partial(_diff_attn_kernel, tq=tq, bk=bk, lam_init=lam_init),
        grid=(b, groups, s // tq),
        in_specs=[pl.BlockSpec((None, tq, width), lambda bi, h, qi: (bi, qi, h)),
                  pl.BlockSpec((None, s, width), lambda bi, h, qi: (bi, 0, groups + h)),
                  pl.BlockSpec((None, s, width), lambda bi, h, qi: (bi, 0, 2 * groups + h)),
                  _resident((1, DIFF_DH)), _resident((1, DIFF_DH)), _resident((1, DIFF_DH)),
                  _resident((1, DIFF_DH)), _resident((1, 2 * DIFF_DH)), _resident(bias.shape)],
        out_specs=pl.BlockSpec((None, tq, width), lambda bi, h, qi: (bi, qi, h)),
        out_shape=jax.ShapeDtypeStruct((b, s, DIFF_WIDTH), BF16),
        scratch_shapes=[pltpu.VMEM((heads, 2 * tq, LANES), F32)] * 3 + [pltpu.VMEM((2, heads, 2 * tq, bk), F32)],
        compiler_params=_cparams(("parallel", "parallel", "arbitrary")),
        name="diff_attn",
    )(proj, proj, proj, vec(lq1), vec(lk1), vec(lq2), vec(lk2), vec(sub_gain), bias)


def _dil_attn_kernel(q_ref, kp_ref, kc_ref, vp_ref, vc_ref, o_ref, k_s, v_s, acc_s, m_s, l_s, bias_s, s_s,
                     q4_s, k4_s, v4_s, acc4_s, m4_s, l4_s):
    first = pl.program_id(2) == 0
    tile = q_ref.shape[0]
    k_s[pl.ds(0, tile), :] = kp_ref[...]
    k_s[pl.ds(tile, tile), :] = kc_ref[...]
    v_s[pl.ds(0, tile), :] = vp_ref[...]
    v_s[pl.ds(tile, tile), :] = vc_ref[...]
    qt, kt = tile // 4, tile // 2
    for rho in range(4):
        q4_s[pl.ds(rho * qt, qt), :] = q_ref[pl.ds(rho, qt, stride=4), :]
        k4_s[pl.ds(rho * kt, qt), :] = kp_ref[pl.ds(rho, qt, stride=4), :]
        k4_s[pl.ds(rho * kt + qt, qt), :] = kc_ref[pl.ds(rho, qt, stride=4), :]
        v4_s[pl.ds(rho * kt, qt), :] = vp_ref[pl.ds(rho, qt, stride=4), :]
        v4_s[pl.ds(rho * kt + qt, qt), :] = vc_ref[pl.ds(rho, qt, stride=4), :]
    scale = jnp.asarray(SWA_DH ** -0.5, BF16)
    lane = lax.broadcasted_iota(jnp.int32, (N_BACK, LANES), 1)
    lo = lane < 64
    r = lax.broadcasted_iota(jnp.int32, (2 * N_BACK, 2 * N_BACK), 0)
    qidx = jnp.where(r >= N_BACK, r - N_BACK, r)
    kk = lax.broadcasted_iota(jnp.int32, (2 * N_BACK, 2 * N_BACK), 1)
    band = (kk >= qidx) & (kk <= qidx + N_BACK)
    bias_s[0] = jnp.where(band, 0.0, NEG_BIG)
    bias_s[1] = jnp.where(band & jnp.logical_or(jnp.logical_not(first), kk >= N_BACK), 0.0, NEG_BIG)

    def windows(dil, n):
        rho, rest = n % 4, n // 4
        if dil == 16:
            rows = pl.ds(rho * qt + rest, N_BACK, stride=4)
            return (q4_s, rows), (k4_s, v4_s, pl.ds(rho * kt + rest, 2 * N_BACK, stride=4)), None, (state4, rows)
        if dil == 4:
            rows = pl.ds(rho * qt + rest * N_BACK, N_BACK)
            keys = pl.ds(rho * kt + qt - N_BACK + rest * N_BACK, 2 * N_BACK)
            return ((q4_s, rows), (k4_s, v4_s, keys), (state4, rows),
                    (state, pl.ds(rho + rest * 4 * N_BACK, N_BACK, stride=4)))
        rows = pl.ds(n * N_BACK, N_BACK)
        return (q_ref, rows), (k_s, v_s, pl.ds(tile + (n - 1) * N_BACK, 2 * N_BACK)), (state, rows), None

    state, state4 = (acc_s, m_s, l_s), (acc4_s, m4_s, l4_s)

    def score(dil, n, slot):
        (q_buf, rows), (k_buf, _, krows), _, _ = windows(dil, n)
        qs = _stack_heads(q_buf[rows, :].astype(BF16), scale)
        kb = k_buf[krows, :].astype(BF16)
        s_s[slot] = lax.dot_general(qs, kb, (((1,), (1,)), ((), ())), preferred_element_type=F32)

    def finish(dil, n, slot):
        (_, rows), (_, v_buf, krows), old, new = windows(dil, n)
        vb = v_buf[krows, :].astype(BF16)
        s = s_s[slot] + bias_s[1 if n < dil else 0]
        m2 = jnp.max(s, axis=-1, keepdims=True)
        p = jnp.exp(s - m2)
        l2 = jnp.sum(p, axis=-1, keepdims=True)
        pv = jnp.dot(p.astype(BF16), vb, preferred_element_type=F32)
        acc_c = jnp.where(lo, pv[:N_BACK], pv[N_BACK:])
        m_c = jnp.where(lo, m2[:N_BACK], m2[N_BACK:])
        l_c = jnp.where(lo, l2[:N_BACK], l2[N_BACK:])
        if old is not None:
            (acc_o, m_o, l_o), orows = old
            m_o = m_o[orows, :]
            m_n = jnp.maximum(m_o, m_c)
            a_o = jnp.exp(m_o - m_n)
            a_c = jnp.exp(m_c - m_n)
            acc_c = a_o * acc_o[orows, :] + a_c * acc_c
            l_c = a_o * l_o[orows, :] + a_c * l_c
            m_c = m_n
        if new is None:
            o_ref[rows, :] = (acc_c / l_c).astype(o_ref.dtype)
        else:
            (acc_n, m_n, l_n), nrows = new
            acc_n[nrows, :] = acc_c
            m_n[nrows, :] = m_c
            l_n[nrows, :] = l_c

    groups = [(dil, g) for dil in DILATIONS for g in range(DIL_TILE // N_BACK // DIL_GROUP)]

    def score_group(i):
        dil, g = groups[i]
        for e in range(DIL_GROUP):
            score(dil, g * DIL_GROUP + e, (i % 2) * DIL_GROUP + e)

    score_group(0)
    for i, (dil, g) in enumerate(groups):
        if i + 1 < len(groups):
            score_group(i + 1)
        for e in range(DIL_GROUP):
            finish(dil, g * DIL_GROUP + e, (i % 2) * DIL_GROUP + e)


def _dil_attn(proj):
    b, s, _ = proj.shape
    pairs = SWA_WIDTH // LANES
    cur = lambda part: pl.BlockSpec((None, DIL_TILE, LANES), lambda bi, hp, i: (bi, i, part * pairs + hp))
    prev = lambda part: pl.BlockSpec((None, DIL_TILE, LANES),
                                     lambda bi, hp, i: (bi, jnp.maximum(i - 1, 0), part * pairs + hp))
    return pl.pallas_call(
        _dil_attn_kernel,
        grid=(b, pairs, s // DIL_TILE),
        in_specs=[cur(0), prev(1), cur(1), prev(2), cur(2)],
        out_specs=pl.BlockSpec((None, DIL_TILE, LANES), lambda bi, hp, i: (bi, i, hp)),
        out_shape=jax.ShapeDtypeStruct((b, s, SWA_WIDTH), BF16),
        scratch_shapes=[pltpu.VMEM((2 * DIL_TILE, LANES), F32), pltpu.VMEM((2 * DIL_TILE, LANES), F32),
                        pltpu.VMEM((DIL_TILE, LANES), F32), pltpu.VMEM((DIL_TILE, LANES), F32),
                        pltpu.VMEM((DIL_TILE, LANES), F32), pltpu.VMEM((2, 2 * N_BACK, 2 * N_BACK), F32),
                        pltpu.VMEM((2 * DIL_GROUP, 2 * N_BACK, 2 * N_BACK), F32),
                        pltpu.VMEM((DIL_TILE, LANES), F32), pltpu.VMEM((2 * DIL_TILE, LANES), F32),
                        pltpu.VMEM((2 * DIL_TILE, LANES), F32), pltpu.VMEM((DIL_TILE, LANES), F32),
                        pltpu.VMEM((DIL_TILE, LANES), F32), pltpu.VMEM((DIL_TILE, LANES), F32)],
        compiler_params=_cparams(("parallel", "parallel", "arbitrary")),
        name="dilated_attn",
    )(proj, proj, proj, proj, proj)


def _bundle_block_diag(x):
    nb, gb = SSM_BUNDLES, SSM_GROUPS // SSM_BUNDLES
    _, c, p = x.shape
    x = x.reshape(nb, gb, c, p)
    out = jnp.zeros((nb, gb, c, gb, p), x.dtype)
    for g in range(gb):
        out = out.at[:, g, :, g, :].set(x[:, g])
    return out.reshape(nb, gb * c, gb * p)


def _s5_prep_kernel(bb_re_ref, bb_im_ref, c_re_ref, c_im_ref, pw_re_ref, pw_im_ref, toep_ref, ts_ref, fs_ref):
    lc = SSM_CHUNK
    hp = lax.Precision.HIGHEST
    nt = (((1,), (1,)), ((), ()))
    bb_re, bb_im, c_re, c_im = bb_re_ref[...], bb_im_ref[...], c_re_ref[...], c_im_ref[...]
    half = bb_re.shape[1]
    blk = lambda i: pl.ds(i * LANES, LANES)
    zero = jnp.zeros((LANES, LANES), toep_ref.dtype)
    for j in range(lc):
        a_re, a_im = pw_re_ref[j:j + 1, :], pw_im_ref[j:j + 1, :]
        ab_re = bb_re * a_re - bb_im * a_im
        ab_im = bb_re * a_im + bb_im * a_re
        lag = (lax.dot_general(ab_re, c_re, nt, precision=hp, preferred_element_type=F32)
               - lax.dot_general(ab_im, c_im, nt, precision=hp, preferred_element_type=F32))
        lag = lag.astype(toep_ref.dtype)
        for s in range(lc - j):
            toep_ref[blk(s), blk(s + j)] = lag
        for t in range(j):
            toep_ref[blk(j), blk(t)] = zero
        s = lc - 1 - j
        ts_ref[blk(s), pl.ds(0, half)] = ab_re.astype(ts_ref.dtype)
        ts_ref[blk(s), pl.ds(half, half)] = ab_im.astype(ts_ref.dtype)
    for t in range(lc):
        a_re, a_im = pw_re_ref[t + 1:t + 2, :], pw_im_ref[t + 1:t + 2, :]
        ca_re = c_re * a_re - c_im * a_im
        ca_im = c_re * a_im + c_im * a_re
        fs_ref[pl.ds(0, half), blk(t)] = ca_re.T.astype(fs_ref.dtype)
        fs_ref[pl.ds(half, half), blk(t)] = (-ca_im).T.astype(fs_ref.dtype)


def _s5_operators(lam_re, lam_im, log_dt, b_re, b_im, c_re, c_im, d_skip):
    lc = SSM_CHUNK
    nb, gb, st = SSM_BUNDLES, SSM_GROUPS // SSM_BUNDLES, SSM_STATE
    lr, li = lam_re.astype(F32), lam_im.astype(F32)
    dt = jnp.exp(log_dt.astype(F32))[:, None]
    mag = jnp.exp(lr * dt)
    abar_re, abar_im = mag * jnp.cos(li * dt), mag * jnp.sin(li * dt)
    den = lr * lr + li * li
    nr, ni = abar_re - 1.0, abar_im
    coef_re = (nr * lr + ni * li) / den
    coef_im = (ni * lr - nr * li) / den
    br, bi = b_re.astype(F32), b_im.astype(F32)
    bb_re = coef_re[..., None] * br - coef_im[..., None] * bi
    bb_im = coef_re[..., None] * bi + coef_im[..., None] * br
    j = jnp.arange(lc + 1, dtype=F32)[:, None, None]
    pmag = jnp.exp(j * (lr * dt)[None])
    pw_re, pw_im = pmag * jnp.cos(j * (li * dt)[None]), pmag * jnp.sin(j * (li * dt)[None])
    per_bundle = lambda pw: jnp.transpose(pw.reshape(lc + 1, nb, gb * st), (1, 0, 2))
    pw_re, pw_im = per_bundle(pw_re), per_bundle(pw_im)
    small = [_bundle_block_diag(jnp.swapaxes(bb_re, 1, 2)), _bundle_block_diag(jnp.swapaxes(bb_im, 1, 2)),
             _bundle_block_diag(c_re.astype(F32)), _bundle_block_diag(c_im.astype(F32)), pw_re, pw_im]
    width, states = lc * LANES, 2 * gb * st
    bundle = lambda shape: pl.BlockSpec((None,) + shape, lambda bi: (bi, 0, 0))
    toep, to_state, from_state = pl.pallas_call(
        _s5_prep_kernel,
        grid=(nb,),
        in_specs=[bundle(a.shape[1:]) for a in small],
        out_specs=[bundle((width, width)), bundle((width, states)), bundle((states, width))],
        out_shape=[jax.ShapeDtypeStruct((nb, width, width), BF16), jax.ShapeDtypeStruct((nb, width, states), BF16),
                   jax.ShapeDtypeStruct((nb, states, width), BF16)],
        compiler_params=_cparams(("parallel",)),
        name="s5_prep",
    )(*small)
    a_lc_re, a_lc_im = pw_re[:, lc], pw_im[:, lc]
    dec_a = jnp.concatenate([a_lc_re, a_lc_re], axis=-1)[:, None, :]
    dec_b = jnp.concatenate([-a_lc_im, a_lc_im], axis=-1)[:, None, :]
    skip = d_skip.astype(F32).reshape(nb, 1, LANES)
    return toep, to_state, from_state, dec_a, dec_b, skip


def _s5_kernel(u_ref, toep_ref, ts_ref, fs_ref, da_ref, db_ref, skip_ref, y_ref, lhs_s, ug_s, loc_s, cin_s,
               state_s):
    lc = SSM_CHUNK
    batch, tile, _ = u_ref.shape
    chunks = tile // lc
    slabs = state_s.shape[0]

    for b in range(batch):
        for s in range(lc):
            piece = u_ref[b, pl.ds(s, chunks, stride=lc), :]
            ug_s[s, pl.ds(b * chunks, chunks), :] = piece
            lhs_s[pl.ds(b * chunks, chunks), pl.ds(s * LANES, LANES)] = piece.astype(BF16)
    quarter = lc * LANES // 4
    ys = [jnp.dot(lhs_s[:, pl.ds(0, (i + 1) * quarter)], toep_ref[pl.ds(0, (i + 1) * quarter), pl.ds(i * quarter, quarter)],
                  preferred_element_type=F32) for i in range(4)]
    loc = jnp.dot(lhs_s[...], ts_ref[...], preferred_element_type=F32)
    for k in range(slabs):
        for b in range(batch):
            loc_s[k, pl.ds(b, chunks, stride=batch), :] = loc[b * chunks:(b + 1) * chunks, k * LANES:(k + 1) * LANES]

    @pl.when(pl.program_id(1) == 0)
    def _():
        state_s[...] = jnp.zeros(state_s.shape, F32)

    da = [da_ref[:, k * LANES:(k + 1) * LANES] for k in range(slabs)]
    db = [db_ref[:, k * LANES:(k + 1) * LANES] for k in range(slabs)]

    def step(n, x):
        rows = pl.ds(pl.multiple_of(n * batch, batch), batch)
        for k in range(slabs):
            cin_s[k, rows, :] = x[k]
        return tuple(da[k] * x[k] + db[k] * x[(k + slabs // 2) % slabs] + loc_s[k, rows, :] for k in range(slabs))

    x = lax.fori_loop(0, chunks, step, tuple(state_s[k] for k in range(slabs)))
    for k in range(slabs):
        state_s[k] = x[k]
    cin = jnp.concatenate(
        [jnp.concatenate([cin_s[k, pl.ds(b, chunks, stride=batch), :] for b in range(batch)], axis=0)
         for k in range(slabs)], axis=1).astype(BF16)
    ys = [ys[i] + jnp.dot(cin, fs_ref[:, pl.ds(i * quarter, quarter)], preferred_element_type=F32) for i in range(4)]
    skip = skip_ref[...]
    per_quarter = lc // 4
    for b in range(batch):
        for t in range(lc):
            rows = pl.ds(b * chunks, chunks)
            cols = (t % per_quarter) * LANES
            y_ref[b, pl.ds(t, chunks, stride=lc), :] = (ys[t // per_quarter][b * chunks:(b + 1) * chunks, cols:cols + LANES]
                                                        + ug_s[t, rows, :] * skip)


def _s5_scan(u, ops, tile):
    toep, to_state, from_state, dec_a, dec_b, skip = ops
    batch, seq, _ = u.shape
    lc = SSM_CHUNK
    rows = batch * (tile // lc)
    slabs = to_state.shape[2] // LANES
    act = pl.BlockSpec((batch, tile, LANES), lambda bi, si: (0, si, bi))
    per_bundle = lambda a, **kw: pl.BlockSpec((None,) + a.shape[1:], lambda bi, si: (bi, 0, 0), **kw)
    once = dict(pipeline_mode=pl.Buffered(1))
    return pl.pallas_call(
        _s5_kernel,
        grid=(SSM_BUNDLES, seq // tile),
        in_specs=[act, per_bundle(toep, **once), per_bundle(to_state, **once), per_bundle(from_state, **once),
                  per_bundle(dec_a), per_bundle(dec_b), per_bundle(skip)],
        out_specs=act,
        out_shape=jax.ShapeDtypeStruct(u.shape, F32),
        scratch_shapes=[pltpu.VMEM((rows, lc * LANES), BF16), pltpu.VMEM((lc, rows, LANES), F32),
                        pltpu.VMEM((slabs, rows, LANES), F32), pltpu.VMEM((slabs, rows, LANES), F32),
                        pltpu.VMEM((slabs, batch, LANES), F32)],
        compiler_params=_cparams(("parallel", "arbitrary")),
        name="s5_scan",
    )(u, toep, to_state, from_state, dec_a, dec_b, skip)


def kernel(x, p, norm_mix, norm_mlp, norm_ple, w_mlp_in, w_mlp_out, w_ple_proj, w_ple_gate, attn_w_in, attn_w_out, diff_lq1, diff_lk1, diff_lq2, diff_lk2, diff_sub_gain, rc_w_in, rc_w_out, ssm_lambda_re, ssm_lambda_im, ssm_log_dt, ssm_b_re, ssm_b_im, ssm_c_re, ssm_c_im, ssm_d, ssm_w_glu, conv_w, norm_final):
    b, s, d = x.shape
    depth = p.shape[0]
    t = b * s
    tm = 512
    assert s % DIL_TILE == 0 and s % tm == 0 and d == 2 * DIFF_WIDTH
    h = x.reshape(t, d)
    for i in range(depth):
        if i % 2 == 0:
            e = i // 2
            lam_init = 0.8 - 0.6 * math.exp(-0.3 * i)
            pa, pb = _norm_matmul(h, norm_mix[i], attn_w_in[e].astype(BF16),
                                  ((0, 3 * DIFF_WIDTH), (3 * DIFF_WIDTH, 3 * DIFF_WIDTH + 3 * SWA_WIDTH)),
                                  (BF16, F32), tm)
            y1 = _diff_attn(pa.reshape(b, s, -1), diff_lq1[e], diff_lk1[e], diff_lq2[e], diff_lk2[e],
                            diff_sub_gain[e], lam_init, tq=512, bk=512, heads=2).reshape(t, -1)
            y2 = _dil_attn(pb.reshape(b, s, -1)).reshape(t, -1)
            wo, w_glu = attn_w_out[e], None
        else:
            o = i // 2
            u, y2 = _ssm_conv_inproj(h, norm_mix[i], rc_w_in[o].astype(BF16), conv_w[o].astype(F32), s, tm)
            ops = _s5_operators(ssm_lambda_re[o], ssm_lambda_im[o], ssm_log_dt[o], ssm_b_re[o], ssm_b_im[o],
                                ssm_c_re[o], ssm_c_im[o], ssm_d[o])
            y1 = _s5_scan(u.reshape(b, s, -1), ops, min(SSM_TILE, s)).reshape(t, -1)
            wo, w_glu = rc_w_out[o], ssm_w_glu[o].astype(BF16)
        h = _post(h, y1, y2, wo.astype(BF16), norm_mlp[i], w_mlp_in[i].astype(BF16), w_mlp_out[i].astype(BF16),
                  norm_ple[i], p.reshape(depth * t, -1), i, w_ple_proj[i].astype(BF16), w_ple_gate[i].astype(BF16),
                  norm_final, i == depth - 1, tm, w_glu)
    return h.reshape(b, s, d)
```

```python
import functools
import math

import jax
import jax.numpy as jnp
from jax import lax
from jax.experimental import pallas as pl
from jax.experimental.pallas import tpu as pltpu

F32 = jnp.float32
BF16 = jnp.bfloat16

EPS = 1e-6
PLE_DIM = 256
DIFF_DH = 64
DIFF_HEADS = 4
DIFF_WIDTH = 512
SWA_DH = 64
SWA_WIDTH = 512
DILATIONS = (16, 4, 1)
N_BACK = 128
DIL_TILE = N_BACK * 16
DIL_GROUP = 2
SSM_GROUP_CH = 16
SSM_STATE = 64
SSM_WIDTH = 512
SSM_GROUPS = 32
SSM_BUNDLES = 4
SSM_CHUNK = 8
SSM_TILE = 1024
CONV_WIDTH = 512
LANES = 128
NEG_BIG = -1e30
VMEM_LIMIT = 60 * 1024 * 1024


def _cparams(sem):
    return pltpu.CompilerParams(dimension_semantics=sem, vmem_limit_bytes=VMEM_LIMIT)


def _resident(shape):
    nd = len(shape)
    return pl.BlockSpec(shape, lambda *_: (0,) * nd, pipeline_mode=pl.Buffered(1))


def _rms(x, g, eps=EPS):
    return x * lax.rsqrt(jnp.mean(x * x, axis=-1, keepdims=True) + eps) * g


def _sigmoid(x):
    return 1.0 / (1.0 + jnp.exp(-x))


def _gelu_tanh(x):
    c = math.sqrt(2.0 / math.pi)
    return x * (0.5 * (1.0 + jnp.tanh(c * (x + 0.044715 * (x * x * x)))))


def _norm_matmul_kernel(h_ref, g_ref, w_ref, *o_refs, splits):
    xn = _rms(h_ref[...], g_ref[...]).astype(BF16)
    for o_ref, (c0, c1) in zip(o_refs, splits):
        o_ref[...] = jnp.dot(xn, w_ref[:, c0:c1], preferred_element_type=F32).astype(o_ref.dtype)


def _norm_matmul(h, g, w, splits, dtypes, tm):
    t, d = h.shape
    n = w.shape[1]
    return pl.pallas_call(
        functools.partial(_norm_matmul_kernel, splits=splits),
        grid=(t // tm,),
        in_specs=[pl.BlockSpec((tm, d), lambda i: (i, 0)), _resident((1, d)), _resident((d, n))],
        out_specs=[pl.BlockSpec((tm, c1 - c0), lambda i: (i, 0)) for (c0, c1) in splits],
        out_shape=[jax.ShapeDtypeStruct((t, c1 - c0), dt) for (c0, c1), dt in zip(splits, dtypes)],
        compiler_params=_cparams(("parallel",)),
        name="norm_matmul",
    )(h, g.reshape(1, d), w)


def _ssm_conv_inproj_kernel(h_ref, hprev_ref, g_ref, w_ref, cw_ref, u_ref, yd_ref, *, tiles_per_seq):
    w = CONV_WIDTH
    g = g_ref[...]
    xn = _rms(h_ref[...], g).astype(BF16)
    z = (jnp.dot(xn, w_ref[:, SSM_WIDTH + w:SSM_WIDTH + 2 * w], preferred_element_type=F32)
         * jnp.dot(xn, w_ref[:, SSM_WIDTH + 2 * w:], preferred_element_type=F32))
    gb = jnp.dot(xn, w_ref[:, SSM_WIDTH:SSM_WIDTH + w], preferred_element_type=F32)
    u_ref[...] = jnp.dot(xn, w_ref[:, :SSM_WIDTH], preferred_element_type=F32)
    prev = jnp.dot(_rms(hprev_ref[...], g).astype(BF16), w_ref[:, SSM_WIDTH + w:], preferred_element_type=F32)
    seq_start = (pl.program_id(0) % tiles_per_seq) == 0
    zh = jnp.where(seq_start, 0.0, prev[:, :w] * prev[:, w:])
    row = lax.broadcasted_iota(jnp.int32, z.shape, 0)
    z1 = jnp.where(row == 0, zh[7:8, :], pltpu.roll(z, 1, 0))
    z2 = jnp.where(row == 0, zh[6:7, :], jnp.where(row == 1, zh[7:8, :], pltpu.roll(z, 2, 0)))
    cw = cw_ref[...]
    yd_ref[...] = (gb * (cw[0:1, :] * z + cw[1:2, :] * z1 + cw[2:3, :] * z2)).astype(yd_ref.dtype)


def _ssm_conv_inproj(h, g, w, conv_w, seq, tm):
    t, d = h.shape
    halo = 8
    return pl.pallas_call(
        functools.partial(_ssm_conv_inproj_kernel, tiles_per_seq=seq // tm),
        grid=(t // tm,),
        in_specs=[pl.BlockSpec((tm, d), lambda i: (i, 0)),
                  pl.BlockSpec((halo, d), lambda i: (jnp.maximum(i * (tm // halo) - 1, 0), 0)),
                  _resident((1, d)), _resident(w.shape), _resident(conv_w.shape)],
        out_specs=[pl.BlockSpec((tm, SSM_WIDTH), lambda i: (i, 0)), pl.BlockSpec((tm, CONV_WIDTH), lambda i: (i, 0))],
        out_shape=[jax.ShapeDtypeStruct((t, SSM_WIDTH), F32), jax.ShapeDtypeStruct((t, CONV_WIDTH), BF16)],
        compiler_params=_cparams(("parallel",)),
        name="ssm_conv_inproj",
    )(h, h, g.reshape(1, d), w, conv_w)


def _post_kernel(h_ref, y1_ref, y2_ref, wo_ref, gm_ref, w1_ref, w2_ref, gp_ref, p_ref, wp_ref, wg_ref,
                 gf_ref, *rest, ff_chunk, final, glu):
    o_ref = rest[-1]
    k1 = y1_ref.shape[1]
    h = h_ref[...]
    y1 = y1_ref[...]
    if glu:
        y1 = _gelu_tanh(y1)
        y1 = (y1 * _sigmoid(jnp.dot(y1.astype(BF16), rest[0][...], preferred_element_type=F32))).astype(BF16)
    h = h + jnp.dot(y1, wo_ref[:k1, :], preferred_element_type=F32)
    h = h + jnp.dot(y2_ref[...], wo_ref[k1:, :], preferred_element_type=F32)
    hn = _rms(h, gm_ref[...]).astype(BF16)
    d_ff = w1_ref.shape[1]
    acc = jnp.zeros_like(h)
    for c in range(d_ff // ff_chunk):
        a = jnp.dot(hn, w1_ref[:, c * ff_chunk:(c + 1) * ff_chunk], preferred_element_type=F32)
        a = jnp.square(jnp.maximum(a, 0.0)).astype(BF16)
        acc = acc + jnp.dot(a, w2_ref[c * ff_chunk:(c + 1) * ff_chunk, :], preferred_element_type=F32)
    h = h + acc
    hn = _rms(h, gp_ref[...]).astype(BF16)
    gate = _sigmoid(jnp.dot(hn, wg_ref[...], preferred_element_type=F32))
    emb = jnp.dot(p_ref[...].astype(BF16), wp_ref[...], preferred_element_type=F32)
    h = h + emb * gate
    if final:
        h = _rms(h, gf_ref[...])
    o_ref[...] = h


def _post(h, y1, y2, wo, gm, w1, w2, gp, p, layer, wp, wg, gf, final, tm, w_glu=None):
    t, d = h.shape
    row = lambda width: pl.BlockSpec((tm, width), lambda i: (i, 0))
    p_row = pl.BlockSpec((tm, p.shape[1]), lambda i: (layer * (t // tm) + i, 0))
    glu = w_glu is not None
    return pl.pallas_call(
        functools.partial(_post_kernel, ff_chunk=1024, final=final, glu=glu),
        grid=(t // tm,),
        in_specs=[row(d), row(y1.shape[1]), row(y2.shape[1]), _resident(wo.shape), _resident((1, d)),
                  _resident(w1.shape), _resident(w2.shape), _resident((1, d)), p_row,
                  _resident(wp.shape), _resident(wg.shape), _resident((1, d))]
                 + ([_resident(w_glu.shape)] if glu else []),
        out_specs=row(d),
        out_shape=jax.ShapeDtypeStruct((t, d), F32),
        compiler_params=_cparams(("parallel",)),
        name="post_mixer",
    )(h, y1, y2, wo, gm.reshape(1, d), w1, w2, gp.reshape(1, d), p, wp, wg, gf.reshape(1, d),
      *([w_glu] if glu else []))


def _stack_heads(x, scale):
    lane = lax.broadcasted_iota(jnp.int32, x.shape, 1)
    zero = jnp.zeros_like(x)
    lo = jnp.where(lane < 64, x, zero)
    hi = jnp.where(lane >= 64, x, zero)
    return jnp.concatenate([lo, hi], axis=0) * scale


def _diff_attn_kernel(q_ref, k_ref, v_ref, lq1_ref, lk1_ref, lq2_ref, lk2_ref, sg_ref, bias_ref, o_ref,
                      m_s, l_s, acc_s, s_s, *, tq, bk, lam_init):
    qi = pl.program_id(2)
    heads = m_s.shape[0]
    lanes = lambda h: pl.ds(h * LANES, LANES)
    scale = jnp.asarray(DIFF_DH ** -0.5, BF16)
    qs = [_stack_heads(q_ref[:, lanes(h)], scale) for h in range(heads)]
    n_full = (qi * tq) // bk

    def scores(j, slot):
        keys = pl.ds(pl.multiple_of(j * bk, bk), bk)
        for h in range(heads):
            s_s[slot, h] = lax.dot_general(qs[h], k_ref[keys, lanes(h)], (((1,), (1,)), ((), ())),
                                           preferred_element_type=F32)

    def update(j, slot, masked):
        keys = pl.ds(pl.multiple_of(j * bk, bk), bk)
        for h in range(heads):
            s = s_s[slot, h]
            if masked:
                s = s + bias_ref[(qi * tq - j * bk) // tq]
            m = m_s[h]
            m_new = jnp.maximum(m, jnp.max(s, axis=-1, keepdims=True))
            alpha = jnp.exp(m - m_new)
            p = jnp.exp(s - jnp.concatenate([m_new] * (bk // LANES), axis=1))
            m_s[h] = m_new
            l_s[h] = alpha * l_s[h] + sum(p[:, c * LANES:(c + 1) * LANES] for c in range(bk // LANES))
            acc_s[h] = alpha * acc_s[h] + jnp.dot(p.astype(BF16), v_ref[keys, lanes(h)],
                                                  preferred_element_type=F32)

    m_s[...] = jnp.full(m_s.shape, NEG_BIG, F32)
    l_s[...] = jnp.zeros(l_s.shape, F32)
    acc_s[...] = jnp.zeros(acc_s.shape, F32)

    scores(0, 0)

    def pair(i, carry):
        j = 2 * i
        scores(j + 1, 1)
        update(j, 0, False)
        scores(j + 2, 0)
        update(j + 1, 1, False)
        return carry

    lax.fori_loop(0, n_full // 2, pair, 0)
    odd = n_full % 2 == 1

    @pl.when(odd)
    def _():
        scores(n_full, 1)
        update(n_full - 1, 0, False)
        update(n_full, 1, True)

    @pl.when(jnp.logical_not(odd))
    def _():
        update(n_full, 0, True)
    lam = (jnp.exp(jnp.sum(lq1_ref[...] * lk1_ref[...], keepdims=True))
           - jnp.exp(jnp.sum(lq2_ref[...] * lk2_ref[...], keepdims=True)) + lam_init)
    for h in range(heads):
        o = acc_s[h] / jnp.sum(l_s[h], axis=-1, keepdims=True)
        o = o[:tq] - lam * o[tq:]
        o = o * lax.rsqrt(jnp.mean(o * o, axis=-1, keepdims=True) + 1e-5) * sg_ref[...]
        o_ref[:, lanes(h)] = (o * (1.0 - lam_init)).astype(o_ref.dtype)


def _diff_attn(proj, lq1, lk1, lq2, lk2, sub_gain, lam_init, tq, bk, heads):
    b, s, _ = proj.shape
    groups, width = DIFF_HEADS // heads, heads * LANES
    vec = lambda a: a.astype(F32).reshape(1, -1)
    off = jnp.arange(bk // tq)[:, None, None] * tq
    qpos = off + (jnp.arange(2 * tq) % tq)[None, :, None]
    bias = jnp.where(jnp.arange(bk)[None, None, :] <= qpos, 0.0, NEG_BIG).astype(F32)
    return pl.pallas_call(
        functools.partial(_diff_attn_kernel, tq=tq, bk=bk, lam_init=lam_init),
        grid=(b, groups, s // tq),
        in_specs=[pl.BlockSpec((None, tq, width), lambda bi, h, qi: (bi, qi, h)),
                  pl.BlockSpec((None, s, width), lambda bi, h, qi: (bi, 0, groups + h)),
                  pl.BlockSpec((None, s, width), lambda bi, h, qi: (bi, 0, 2 * groups + h)),
                  _resident((1, DIFF_DH)), _resident((1, DIFF_DH)), _resident((1, DIFF_DH)),
                  _resident((1, DIFF_DH)), _resident((1, 2 * DIFF_DH)), _resident(bias.shape)],
        out_specs=pl.BlockSpec((None, tq, width), lambda bi, h, qi: (bi, qi, h)),
        out_shape=jax.ShapeDtypeStruct((b, s, DIFF_WIDTH), BF16),
        scratch_shapes=[pltpu.VMEM((heads, 2 * tq, LANES), F32)] * 3 + [pltpu.VMEM((2, heads, 2 * tq, bk), F32)],
        compiler_params=_cparams(("parallel", "parallel", "arbitrary")),
        name="diff_attn",
    )(proj, proj, proj, vec(lq1), vec(lk1), vec(lq2), vec(lk2), vec(sub_gain), bias)


def _dil_attn_kernel(q_ref, kp_ref, kc_ref, vp_ref, vc_ref, o_ref, k_s, v_s, acc_s, m_s, l_s, bias_s, s_s,
                     q4_s, k4_s, v4_s, acc4_s, m4_s, l4_s):
    first = pl.program_id(2) == 0
    tile = q_ref.shape[0]
    k_s[pl.ds(0, tile), :] = kp_ref[...]
    k_s[pl.ds(tile, tile), :] = kc_ref[...]
    v_s[pl.ds(0, tile), :] = vp_ref[...]
    v_s[pl.ds(tile, tile), :] = vc_ref[...]
    qt, kt = tile // 4, tile // 2
    for rho in range(4):
        q4_s[pl.ds(rho * qt, qt), :] = q_ref[pl.ds(rho, qt, stride=4), :]
        k4_s[pl.ds(rho * kt, qt), :] = kp_ref[pl.ds(rho, qt, stride=4), :]
        k4_s[pl.ds(rho * kt + qt, qt), :] = kc_ref[pl.ds(rho, qt, stride=4), :]
        v4_s[pl.ds(rho * kt, qt), :] = vp_ref[pl.ds(rho, qt, stride=4), :]
        v4_s[pl.ds(rho * kt + qt, qt), :] = vc_ref[pl.ds(rho, qt, stride=4), :]
    scale = jnp.asarray(SWA_DH ** -0.5, BF16)
    lane = lax.broadcasted_iota(jnp.int32, (N_BACK, LANES), 1)
    lo = lane < 64
    r = lax.broadcasted_iota(jnp.int32, (2 * N_BACK, 2 * N_BACK), 0)
    qidx = jnp.where(r >= N_BACK, r - N_BACK, r)
    kk = lax.broadcasted_iota(jnp.int32, (2 * N_BACK, 2 * N_BACK), 1)
    band = (kk >= qidx) & (kk <= qidx + N_BACK)
    bias_s[0] = jnp.where(band, 0.0, NEG_BIG)
    bias_s[1] = jnp.where(band & jnp.logical_or(jnp.logical_not(first), kk >= N_BACK), 0.0, NEG_BIG)

    def windows(dil, n):
        rho, rest = n % 4, n // 4
        if dil == 16:
            rows = pl.ds(rho * qt + rest, N_BACK, stride=4)
            return (q4_s, rows), (k4_s, v4_s, pl.ds(rho * kt + rest, 2 * N_BACK, stride=4)), None, (state4, rows)
        if dil == 4:
            rows = pl.ds(rho * qt + rest * N_BACK, N_BACK)
            keys = pl.ds(rho * kt + qt - N_BACK + rest * N_BACK, 2 * N_BACK)
            return ((q4_s, rows), (k4_s, v4_s, keys), (state4, rows),
                    (state, pl.ds(rho + rest * 4 * N_BACK, N_BACK, stride=4)))
        rows = pl.ds(n * N_BACK, N_BACK)
        return (q_ref, rows), (k_s, v_s, pl.ds(tile + (n - 1) * N_BACK, 2 * N_BACK)), (state, rows), None

    state, state4 = (acc_s, m_s, l_s), (acc4_s, m4_s, l4_s)

    def score(dil, n, slot):
        (q_buf, rows), (k_buf, _, krows), _, _ = windows(dil, n)
        qs = _stack_heads(q_buf[rows, :].astype(BF16), scale)
        kb = k_buf[krows, :].astype(BF16)
        s_s[slot] = lax.dot_general(qs, kb, (((1,), (1,)), ((), ())), preferred_element_type=F32)

    def finish(dil, n, slot):
        (_, rows), (_, v_buf, krows), old, new = windows(dil, n)
        vb = v_buf[krows, :].astype(BF16)
        s = s_s[slot] + bias_s[1 if n < dil else 0]
        m2 = jnp.max(s, axis=-1, keepdims=True)
        p = jnp.exp(s - m2)
        l2 = jnp.sum(p, axis=-1, keepdims=True)
        pv = jnp.dot(p.astype(BF16), vb, preferred_element_type=F32)
        acc_c = jnp.where(lo, pv[:N_BACK], pv[N_BACK:])
        m_c = jnp.where(lo, m2[:N_BACK], m2[N_BACK:])
        l_c = jnp.where(lo, l2[:N_BACK], l2[N_BACK:])
        if old is not None:
            (acc_o, m_o, l_o), orows = old
            m_o = m_o[orows, :]
            m_n = jnp.maximum(m_o, m_c)
            a_o = jnp.exp(m_o - m_n)
            a_c = jnp.exp(m_c - m_n)
            acc_c = a_o * acc_o[orows, :] + a_c * acc_c
            l_c = a_o * l_o[orows, :] + a_c * l_c
            m_c = m_n
        if new is None:
            o_ref[rows, :] = (acc_c / l_c).astype(o_ref.dtype)
        else:
            (acc_n, m_n, l_n), nrows = new
            acc_n[nrows, :] = acc_c
            m_n[nrows, :] = m_c
            l_n[nrows, :] = l_c

    groups = [(dil, g) for dil in DILATIONS for g in range(DIL_TILE // N_BACK // DIL_GROUP)]

    def score_group(i):
        dil, g = groups[i]
        for e in range(DIL_GROUP):
            score(dil, g * DIL_GROUP + e, (i % 2) * DIL_GROUP + e)

    score_group(0)
    for i, (dil, g) in enumerate(groups):
        if i + 1 < len(groups):
            score_group(i + 1)
        for e in range(DIL_GROUP):
            finish(dil, g * DIL_GROUP + e, (i % 2) * DIL_GROUP + e)


def _dil_attn(proj):
    b, s, _ = proj.shape
    pairs = SWA_WIDTH // LANES
    cur = lambda part: pl.BlockSpec((None, DIL_TILE, LANES), lambda bi, hp, i: (bi, i, part * pairs + hp))
    prev = lambda part: pl.BlockSpec((None, DIL_TILE, LANES),
                                     lambda bi, hp, i: (bi, jnp.maximum(i - 1, 0), part * pairs + hp))
    return pl.pallas_call(
        _dil_attn_kernel,
        grid=(b, pairs, s // DIL_TILE),
        in_specs=[cur(0), prev(1), cur(1), prev(2), cur(2)],
        out_specs=pl.BlockSpec((None, DIL_TILE, LANES), lambda bi, hp, i: (bi, i, hp)),
        out_shape=jax.ShapeDtypeStruct((b, s, SWA_WIDTH), BF16),
        scratch_shapes=[pltpu.VMEM((2 * DIL_TILE, LANES), F32), pltpu.VMEM((2 * DIL_TILE, LANES), F32),
                        pltpu.VMEM((DIL_TILE, LANES), F32), pltpu.VMEM((DIL_TILE, LANES), F32),
                        pltpu.VMEM((DIL_TILE, LANES), F32), pltpu.VMEM((2, 2 * N_BACK, 2 * N_BACK), F32),
                        pltpu.VMEM((2 * DIL_GROUP, 2 * N_BACK, 2 * N_BACK), F32),
                        pltpu.VMEM((DIL_TILE, LANES), F32), pltpu.VMEM((2 * DIL_TILE, LANES), F32),
                        pltpu.VMEM((2 * DIL_TILE, LANES), F32), pltpu.VMEM((DIL_TILE, LANES), F32),
                        pltpu.VMEM((DIL_TILE, LANES), F32), pltpu.VMEM((DIL_TILE, LANES), F32)],
        compiler_params=_cparams(("parallel", "parallel", "arbitrary")),
        name="dilated_attn",
    )(proj, proj, proj, proj, proj)


def _bundle_block_diag(x):
    nb, gb = SSM_BUNDLES, SSM_GROUPS // SSM_BUNDLES
    _, c, p = x.shape
    x = x.reshape(nb, gb, c, p)
    out = jnp.zeros((nb, gb, c, gb, p), x.dtype)
    for g in range(gb):
        out = out.at[:, g, :, g, :].set(x[:, g])
    return out.reshape(nb, gb * c, gb * p)


def _s5_prep_kernel(bb_re_ref, bb_im_ref, c_re_ref, c_im_ref, pw_re_ref, pw_im_ref, toep_ref, ts_ref, fs_ref):
    lc = SSM_CHUNK
    hp = lax.Precision.HIGHEST
    nt = (((1,), (1,)), ((), ()))
    bb_re, bb_im, c_re, c_im = bb_re_ref[...], bb_im_ref[...], c_re_ref[...], c_im_ref[...]
    half = bb_re.shape[1]
    blk = lambda i: pl.ds(i * LANES, LANES)
    zero = jnp.zeros((LANES, LANES), toep_ref.dtype)
    for j in range(lc):
        a_re, a_im = pw_re_ref[j:j + 1, :], pw_im_ref[j:j + 1, :]
        ab_re = bb_re * a_re - bb_im * a_im
        ab_im = bb_re * a_im + bb_im * a_re
        lag = (lax.dot_general(ab_re, c_re, nt, precision=hp, preferred_element_type=F32)
               - lax.dot_general(ab_im, c_im, nt, precision=hp, preferred_element_type=F32))
        lag = lag.astype(toep_ref.dtype)
        for s in range(lc - j):
            toep_ref[blk(s), blk(s + j)] = lag
        for t in range(j):
            toep_ref[blk(j), blk(t)] = zero
        s = lc - 1 - j
        ts_ref[blk(s), pl.ds(0, half)] = ab_re.astype(ts_ref.dtype)
        ts_ref[blk(s), pl.ds(half, half)] = ab_im.astype(ts_ref.dtype)
    for t in range(lc):
        a_re, a_im = pw_re_ref[t + 1:t + 2, :], pw_im_ref[t + 1:t + 2, :]
        ca_re = c_re * a_re - c_im * a_im
        ca_im = c_re * a_im + c_im * a_re
        fs_ref[pl.ds(0, half), blk(t)] = ca_re.T.astype(fs_ref.dtype)
        fs_ref[pl.ds(half, half), blk(t)] = (-ca_im).T.astype(fs_ref.dtype)


def _s5_operators(lam_re, lam_im, log_dt, b_re, b_im, c_re, c_im, d_skip):
    lc = SSM_CHUNK
    nb, gb, st = SSM_BUNDLES, SSM_GROUPS // SSM_BUNDLES, SSM_STATE
    lr, li = lam_re.astype(F32), lam_im.astype(F32)
    dt = jnp.exp(log_dt.astype(F32))[:, None]
    mag = jnp.exp(lr * dt)
    abar_re, abar_im = mag * jnp.cos(li * dt), mag * jnp.sin(li * dt)
    den = lr * lr + li * li
    nr, ni = abar_re - 1.0, abar_im
    coef_re = (nr * lr + ni * li) / den
    coef_im = (ni * lr - nr * li) / den
    br, bi = b_re.astype(F32), b_im.astype(F32)
    bb_re = coef_re[..., None] * br - coef_im[..., None] * bi
    bb_im = coef_re[..., None] * bi + coef_im[..., None] * br
    j = jnp.arange(lc + 1, dtype=F32)[:, None, None]
    pmag = jnp.exp(j * (lr * dt)[None])
    pw_re, pw_im = pmag * jnp.cos(j * (li * dt)[None]), pmag * jnp.sin(j * (li * dt)[None])
    per_bundle = lambda pw: jnp.transpose(pw.reshape(lc + 1, nb, gb * st), (1, 0, 2))
    pw_re, pw_im = per_bundle(pw_re), per_bundle(pw_im)
    small = [_bundle_block_diag(jnp.swapaxes(bb_re, 1, 2)), _bundle_block_diag(jnp.swapaxes(bb_im, 1, 2)),
             _bundle_block_diag(c_re.astype(F32)), _bundle_block_diag(c_im.astype(F32)), pw_re, pw_im]
    width, states = lc * LANES, 2 * gb * st
    bundle = lambda shape: pl.BlockSpec((None,) + shape, lambda bi: (bi, 0, 0))
    toep, to_state, from_state = pl.pallas_call(
        _s5_prep_kernel,
        grid=(nb,),
        in_specs=[bundle(a.shape[1:]) for a in small],
        out_specs=[bundle((width, width)), bundle((width, states)), bundle((states, width))],
        out_shape=[jax.ShapeDtypeStruct((nb, width, width), BF16), jax.ShapeDtypeStruct((nb, width, states), BF16),
                   jax.ShapeDtypeStruct((nb, states, width), BF16)],
        compiler_params=_cparams(("parallel",)),
        name="s5_prep",
    )(*small)
    a_lc_re, a_lc_im = pw_re[:, lc], pw_im[:, lc]
    dec_a = jnp.concatenate([a_lc_re, a_lc_re], axis=-1)[:, None, :]
    dec_b = jnp.concatenate([-a_lc_im, a_lc_im], axis=-1)[:, None, :]
    skip = d_skip.astype(F32).reshape(nb, 1, LANES)
    return toep, to_state, from_state, dec_a, dec_b, skip


def _s5_kernel(u_ref, toep_ref, ts_ref, fs_ref, da_ref, db_ref, skip_ref, y_ref, lhs_s, ug_s, loc_s, cin_s,
               state_s):
    lc = SSM_CHUNK
    batch, tile, _ = u_ref.shape
    chunks = tile // lc
    slabs = state_s.shape[0]

    for b in range(batch):
        for s in range(lc):
            piece = u_ref[b, pl.ds(s, chunks, stride=lc), :]
            ug_s[s, pl.ds(b * chunks, chunks), :] = piece
            lhs_s[pl.ds(b * chunks, chunks), pl.ds(s * LANES, LANES)] = piece.astype(BF16)
    quarter = lc * LANES // 4
    ys = [jnp.dot(lhs_s[:, pl.ds(0, (i + 1) * quarter)], toep_ref[pl.ds(0, (i + 1) * quarter), pl.ds(i * quarter, quarter)],
                  preferred_element_type=F32) for i in range(4)]
    loc = jnp.dot(lhs_s[...], ts_ref[...], preferred_element_type=F32)
    for k in range(slabs):
        for b in range(batch):
            loc_s[k, pl.ds(b, chunks, stride=batch), :] = loc[b * chunks:(b + 1) * chunks, k * LANES:(k + 1) * LANES]

    @pl.when(pl.program_id(1) == 0)
    def _():
        state_s[...] = jnp.zeros(state_s.shape, F32)

    da = [da_ref[:, k * LANES:(k + 1) * LANES] for k in range(slabs)]
    db = [db_ref[:, k * LANES:(k + 1) * LANES] for k in range(slabs)]

    def step(n, x):
        rows = pl.ds(pl.multiple_of(n * batch, batch), batch)
        for k in range(slabs):
            cin_s[k, rows, :] = x[k]
        return tuple(da[k] * x[k] + db[k] * x[(k + slabs // 2) % slabs] + loc_s[k, rows, :] for k in range(slabs))

    x = lax.fori_loop(0, chunks, step, tuple(state_s[k] for k in range(slabs)))
    for k in range(slabs):
        state_s[k] = x[k]
    cin = jnp.concatenate(
        [jnp.concatenate([cin_s[k, pl.ds(b, chunks, stride=batch), :] for b in range(batch)], axis=0)
         for k in range(slabs)], axis=1).astype(BF16)
    ys = [ys[i] + jnp.dot(cin, fs_ref[:, pl.ds(i * quarter, quarter)], preferred_element_type=F32) for i in range(4)]
    skip = skip_ref[...]
    per_quarter = lc // 4
    for b in range(batch):
        for t in range(lc):
            rows = pl.ds(b * chunks, chunks)
            cols = (t % per_quarter) * LANES
            y_ref[b, pl.ds(t, chunks, stride=lc), :] = (ys[t // per_quarter][b * chunks:(b + 1) * chunks, cols:cols + LANES]
                                                        + ug_s[t, rows, :] * skip)


def _s5_scan(u, ops, tile):
    toep, to_state, from_state, dec_a, dec_b, skip = ops
    batch, seq, _ = u.shape
    lc = SSM_CHUNK
    rows = batch * (tile // lc)
    slabs = to_state.shape[2] // LANES
    act = pl.BlockSpec((batch, tile, LANES), lambda bi, si: (0, si, bi))
    per_bundle = lambda a, **kw: pl.BlockSpec((None,) + a.shape[1:], lambda bi, si: (bi, 0, 0), **kw)
    once = dict(pipeline_mode=pl.Buffered(1))
    return pl.pallas_call(
        _s5_kernel,
        grid=(SSM_BUNDLES, seq // tile),
        in_specs=[act, per_bundle(toep, **once), per_bundle(to_state, **once), per_bundle(from_state, **once),
                  per_bundle(dec_a), per_bundle(dec_b), per_bundle(skip)],
        out_specs=act,
        out_shape=jax.ShapeDtypeStruct(u.shape, F32),
        scratch_shapes=[pltpu.VMEM((rows, lc * LANES), BF16), pltpu.VMEM((lc, rows, LANES), F32),
                        pltpu.VMEM((slabs, rows, LANES), F32), pltpu.VMEM((slabs, rows, LANES), F32),
                        pltpu.VMEM((slabs, batch, LANES), F32)],
        compiler_params=_cparams(("parallel", "arbitrary")),
        name="s5_scan",
    )(u, toep, to_state, from_state, dec_a, dec_b, skip)


def kernel(x, p, norm_mix, norm_mlp, norm_ple, w_mlp_in, w_mlp_out, w_ple_proj, w_ple_gate, attn_w_in, attn_w_out, diff_lq1, diff_lk1, diff_lq2, diff_lk2, diff_sub_gain, rc_w_in, rc_w_out, ssm_lambda_re, ssm_lambda_im, ssm_log_dt, ssm_b_re, ssm_b_im, ssm_c_re, ssm_c_im, ssm_d, ssm_w_glu, conv_w, norm_final):
    b, s, d = x.shape
    depth = p.shape[0]
    t = b * s
    tm = 512
    assert s % DIL_TILE == 0 and s % tm == 0 and d == 2 * DIFF_WIDTH
    h = x.reshape(t, d)
    for i in range(depth):
        if i % 2 == 0:
            e = i // 2
            lam_init = 0.8 - 0.6 * math.exp(-0.3 * i)
            pa, pb = _norm_matmul(h, norm_mix[i], attn_w_in[e].astype(BF16),
                                  ((0, 3 * DIFF_WIDTH), (3 * DIFF_WIDTH, 3 * DIFF_WIDTH + 3 * SWA_WIDTH)),
                                  (BF16, F32), tm)
            y1 = _diff_attn(pa.reshape(b, s, -1), diff_lq1[e], diff_lk1[e], diff_lq2[e], diff_lk2[e],
                            diff_sub_gain[e], lam_init, tq=512, bk=512, heads=2).reshape(t, -1)
            y2 = _dil_attn(pb.reshape(b, s, -1)).reshape(t, -1)
            wo, w_glu = attn_w_out[e], None
        else:
            o = i // 2
            u, y2 = _ssm_conv_inproj(h, norm_mix[i], rc_w_in[o].astype(BF16), conv_w[o].astype(F32), s, tm)
            ops = _s5_operators(ssm_lambda_re[o], ssm_lambda_im[o], ssm_log_dt[o], ssm_b_re[o], ssm_b_im[o],
                                ssm_c_re[o], ssm_c_im[o], ssm_d[o])
            y1 = _s5_scan(u.reshape(b, s, -1), ops, min(SSM_TILE, s)).reshape(t, -1)
            wo, w_glu = rc_w_out[o], ssm_w_glu[o].astype(BF16)
        h = _post(h, y1, y2, wo.astype(BF16), norm_mlp[i], w_mlp_in[i].astype(BF16), w_mlp_out[i].astype(BF16),
                  norm_ple[i], p.reshape(depth * t, -1), i, w_ple_proj[i].astype(BF16), w_ple_gate[i].astype(BF16),
                  norm_final, i == depth - 1, tm, w_glu)
    return h.reshape(b, s, d)
```
